```python
import jax, jax.numpy as jnp
from jax import lax
import numpy as np

D_MODEL = 1024
BATCH = 8
SEQ = 2048
DEPTH = 4
DEC_BATCH = 128
DEC_SEQ = 1
PAST_LEN = 16384
PAGE_SIZE = 128

D_RNN = D_MODEL
RG_BLOCKS = 16
RG_BW = D_RNN // RG_BLOCKS
RG_CONV = 4
RG_C = 8.0
GLA_HEADS = 4
GLA_DK = D_MODEL // (2 * GLA_HEADS)
GLA_DV = D_MODEL // GLA_HEADS
QK_DIM = GLA_HEADS * GLA_DK
V_DIM = GLA_HEADS * GLA_DV
GLA_RANK = 16
GLA_TAU = 16.0
GLA_CHUNK = 64
D_FF = 3 * D_MODEL
FFN_CONV = 3
N_META = 16
EPS = 1e-6
IN_SIZES = (D_RNN, D_RNN, QK_DIM, QK_DIM, V_DIM, GLA_RANK, V_DIM, D_MODEL, D_MODEL)
D_IN = sum(IN_SIZES)

kernel_name = "hawk_gla_parallel_convffn_step"


def _rmsnorm(x, g):
    xf = x.astype(jnp.float32)
    y = xf * lax.rsqrt(jnp.mean(xf * xf, axis=-1, keepdims=True) + EPS)
    return (y * g.astype(jnp.float32)).astype(x.dtype)


def _causal_dwconv(x, buf, w, b):
    K = w.shape[0]
    T = x.shape[1]
    xx = jnp.concatenate([buf.astype(x.dtype), x], axis=1)
    y = b + xx[:, 0:T] * w[0]
    for j in range(1, K):
        y = y + xx[:, j:j + T] * w[j]
    return y, xx[:, xx.shape[1] - (K - 1):]


def _lin_comb(left, right):
    a1, b1 = left
    a2, b2 = right
    return a1 * a2, a2 * b1 + b2


def _rglru(x, h0, wa, ba, wx, bx, lam):
    B, T, _ = x.shape
    f32 = jnp.float32
    xf = x.astype(f32)
    xb = xf.reshape(B, T, RG_BLOCKS, RG_BW)
    r = jax.nn.sigmoid(jnp.einsum('btnc,ncd->btnd', xb, wa.astype(f32)).reshape(B, T, D_RNN) + ba.astype(f32))
    i = jax.nn.sigmoid(jnp.einsum('btnc,ncd->btnd', xb, wx.astype(f32)).reshape(B, T, D_RNN) + bx.astype(f32))
    log_a = -RG_C * r * jax.nn.softplus(-lam.astype(f32))
    a = jnp.exp(log_a)
    b = jnp.sqrt(-jnp.expm1(2.0 * log_a)) * (i * xf)
    b = b.at[:, 0].add(a[:, 0] * h0.astype(f32))
    _, h = lax.associative_scan(_lin_comb, (a, b), axis=1)
    return h.astype(x.dtype), h[:, -1].astype(x.dtype)


def _gla(q, k, v, log_alpha, S0, pad_left):
    f32 = jnp.float32
    B, T, H, DK = q.shape
    DV = v.shape[-1]
    C = GLA_CHUNK
    pad_right = (-(pad_left + T)) % C
    pw = ((0, 0), (pad_left, pad_right), (0, 0), (0, 0))
    q, k, v, g = [jnp.pad(t.astype(f32), pw) for t in (q, k, v, log_alpha)]
    Tp = T + pad_left + pad_right
    n = Tp // C

    def blk(t):
        return t.reshape(B, n, C, H, t.shape[-1]).transpose(0, 3, 1, 2, 4)

    q, k, v, g = blk(q), blk(k), blk(v), blk(g)
    bc = jnp.cumsum(g, axis=3)
    bl = bc[:, :, :, -1:]
    qd = q * jnp.exp(bc)
    kd = k * jnp.exp(-bc)
    ke = k * jnp.exp(bl - bc)
    mask = jnp.tril(jnp.ones((C, C), dtype=bool))
    A = jnp.where(mask, jnp.einsum('bhncd,bhnsd->bhncs', qd, kd), 0.0)
    o = jnp.einsum('bhncs,bhnse->bhnce', A, v)
    dS = jnp.einsum('bhncd,bhnce->bhnde', ke, v)
    decay = jnp.exp(bl[:, :, :, 0])

    def step(S, inp):
        dec, ds = inp
        return dec[..., None] * S + ds, S

    S_T, S_starts = lax.scan(step, S0.astype(f32), (jnp.moveaxis(decay, 2, 0), jnp.moveaxis(dS, 2, 0)))
    o = o + jnp.einsum('bhncd,nbhde->bhnce', qd, S_starts)
    o = o.transpose(0, 2, 3, 1, 4).reshape(B, Tp, H, DV)[:, pad_left:pad_left + T]
    return o, S_T


def _layer(x, rg_buf, h0, S0, ffn_buf, pad_left, p):
    B, T, _ = x.shape
    f32 = jnp.float32
    xn = _rmsnorm(x, p['norm1_g'])
    z = jnp.einsum('btd,de->bte', xn, p['w_in'])
    split_points = [int(s) for s in np.cumsum(IN_SIZES)[:-1]]
    rg_x, rg_y, q, k, v, g_lr, g_out, m_a, m_b = jnp.split(z, split_points, axis=-1)
    xc, new_rg_buf = _causal_dwconv(rg_x, rg_buf, p['rg_conv_w'], p['rg_conv_b'])
    h, hT = _rglru(xc, h0, p['rg_wa'], p['rg_ba'], p['rg_wx'], p['rg_bx'], p['rg_lambda'])
    ya = jnp.einsum('btc,cd->btd', h * jax.nn.gelu(rg_y), p['w_branch_a'])
    zg = jnp.einsum('btr,rk->btk', g_lr, p['gla_w_gate']) + p['gla_b_gate']
    log_alpha = (jax.nn.log_sigmoid(zg.astype(f32)) / GLA_TAU).reshape(B, T, GLA_HEADS, GLA_DK)
    qh = q.reshape(B, T, GLA_HEADS, GLA_DK) * (GLA_DK ** -0.5)
    kh = k.reshape(B, T, GLA_HEADS, GLA_DK)
    vh = v.reshape(B, T, GLA_HEADS, GLA_DV)
    o, S_T = _gla(qh, kh, vh, log_alpha, S0, pad_left)
    o = o * lax.rsqrt(jnp.mean(o * o, axis=-1, keepdims=True) + EPS) * p['gla_norm_g'].astype(f32)
    o = o * jax.nn.silu(g_out.reshape(B, T, GLA_HEADS, GLA_DV).astype(f32))
    yb = jnp.einsum('btc,cd->btd', o.astype(x.dtype).reshape(B, T, V_DIM), p['w_branch_b'])
    merged = jax.nn.sigmoid(m_a) * ya + jax.nn.sigmoid(m_b) * yb
    x = x + jnp.einsum('btd,de->bte', merged, p['w_out'])
    xn2 = _rmsnorm(x, p['norm2_g'])
    u = jnp.einsum('btd,df->btf', xn2, p['w_ffn_up'])
    uc, new_ffn_buf = _causal_dwconv(u, ffn_buf, p['ffn_conv_w'], p['ffn_conv_b'])
    gate, val = jnp.split(uc, 2, axis=-1)
    x = x + jnp.einsum('btf,fd->btd', jax.nn.gelu(gate) * val, p['w_ffn_down'])
    return x, new_rg_buf, hT, S_T.astype(x.dtype), new_ffn_buf


def _trunk(x, rg_buf, rg_h, gla_S, ffn_buf, pad_left, p, final_norm_g):
    bufs, hs, Ss, fbufs = [], [], [], []
    for l in range(DEPTH):
        pl = {name: arr[l] for name, arr in p.items()}
        x, nb, nh, nS, nf = _layer(x, rg_buf[l], rg_h[l], gla_S[l], ffn_buf[l], pad_left, pl)
        bufs.append(nb)
        hs.append(nh)
        Ss.append(nS)
        fbufs.append(nf)
    y = _rmsnorm(x, final_norm_g)
    return y, jnp.stack(bufs), jnp.stack(hs), jnp.stack(Ss), jnp.stack(fbufs)


def setup_inputs(seed: int = 0) -> dict:
    key = jax.random.key(seed)
    ks = jax.random.split(key, 32)
    f32 = jnp.float32

    def nrm(k, shape, scale):
        return jax.random.normal(k, shape, f32) * scale

    a0 = jax.random.uniform(ks[12], (DEPTH, D_RNN), f32, 0.9, 0.999)
    return {
        "x_prompt": nrm(ks[0], (BATCH, SEQ, D_MODEL), 1.0),
        "x_sample": nrm(ks[1], (DEC_BATCH, DEC_SEQ, D_MODEL), 1.0),
        "state_rg_conv": nrm(ks[2], (DEPTH, DEC_BATCH, RG_CONV - 1, D_RNN), 1.0),
        "state_rg_h": nrm(ks[3], (DEPTH, DEC_BATCH, D_RNN), 0.5),
        "state_gla": nrm(ks[4], (DEPTH, DEC_BATCH, GLA_HEADS, GLA_DK, GLA_DV), 1.0),
        "state_ffn_conv": nrm(ks[5], (DEPTH, DEC_BATCH, FFN_CONV - 1, 2 * D_FF), 1.0),
        "meta_tokens": nrm(ks[6], (N_META, D_MODEL), 1.0),
        "norm1_g": 1.0 + nrm(ks[7], (DEPTH, D_MODEL), 0.02),
        "w_in": nrm(ks[8], (DEPTH, D_MODEL, D_IN), D_MODEL ** -0.5),
        "rg_conv_w": nrm(ks[9], (DEPTH, RG_CONV, D_RNN), RG_CONV ** -0.5),
        "rg_conv_b": nrm(ks[10], (DEPTH, D_RNN), 0.02),
        "rg_wa": nrm(ks[11], (DEPTH, RG_BLOCKS, RG_BW, RG_BW), RG_BW ** -0.5),
        "rg_ba": nrm(ks[13], (DEPTH, D_RNN), 0.02),
        "rg_wx": nrm(ks[14], (DEPTH, RG_BLOCKS, RG_BW, RG_BW), RG_BW ** -0.5),
        "rg_bx": nrm(ks[15], (DEPTH, D_RNN), 0.02),
        "rg_lambda": jnp.log(a0 / (1.0 - a0)),
        "gla_w_gate": nrm(ks[16], (DEPTH, GLA_RANK, QK_DIM), GLA_RANK ** -0.5),
        "gla_b_gate": nrm(ks[17], (DEPTH, QK_DIM), 0.1),
        "gla_norm_g": 1.0 + nrm(ks[18], (DEPTH, GLA_HEADS, GLA_DV), 0.02),
        "w_branch_a": nrm(ks[19], (DEPTH, D_RNN, D_MODEL), D_RNN ** -0.5),
        "w_branch_b": nrm(ks[20], (DEPTH, V_DIM, D_MODEL), V_DIM ** -0.5),
        "w_out": nrm(ks[21], (DEPTH, D_MODEL, D_MODEL), D_MODEL ** -0.5),
        "norm2_g": 1.0 + nrm(ks[22], (DEPTH, D_MODEL), 0.02),
        "w_ffn_up": nrm(ks[23], (DEPTH, D_MODEL, 2 * D_FF), D_MODEL ** -0.5),
        "ffn_conv_w": nrm(ks[24], (DEPTH, FFN_CONV, 2 * D_FF), FFN_CONV ** -0.5),
        "ffn_conv_b": nrm(ks[25], (DEPTH, 2 * D_FF), 0.02),
        "w_ffn_down": nrm(ks[26], (DEPTH, D_FF, D_MODEL), D_FF ** -0.5),
        "final_norm_g": 1.0 + nrm(ks[27], (D_MODEL,), 0.02),
    }


def reference(x_prompt, x_sample, state_rg_conv, state_rg_h, state_gla, state_ffn_conv,
              meta_tokens, norm1_g, w_in, rg_conv_w, rg_conv_b, rg_wa, rg_ba, rg_wx, rg_bx,
              rg_lambda, gla_w_gate, gla_b_gate, gla_norm_g, w_branch_a, w_branch_b, w_out,
              norm2_g, w_ffn_up, ffn_conv_w, ffn_conv_b, w_ffn_down, final_norm_g):
    p = dict(norm1_g=norm1_g, w_in=w_in, rg_conv_w=rg_conv_w, rg_conv_b=rg_conv_b,
             rg_wa=rg_wa, rg_ba=rg_ba, rg_wx=rg_wx, rg_bx=rg_bx, rg_lambda=rg_lambda,
             gla_w_gate=gla_w_gate, gla_b_gate=gla_b_gate, gla_norm_g=gla_norm_g,
             w_branch_a=w_branch_a, w_branch_b=w_branch_b, w_out=w_out, norm2_g=norm2_g,
             w_ffn_up=w_ffn_up, ffn_conv_w=ffn_conv_w, ffn_conv_b=ffn_conv_b,
             w_ffn_down=w_ffn_down)
    dt = x_prompt.dtype
    B = x_prompt.shape[0]
    xp = jnp.concatenate([jnp.broadcast_to(meta_tokens.astype(dt)[None], (B, N_META, D_MODEL)), x_prompt], axis=1)
    zeros_rg_buf = jnp.zeros((DEPTH, B, RG_CONV - 1, D_RNN), dt)
    zeros_h = jnp.zeros((DEPTH, B, D_RNN), dt)
    zeros_S = jnp.zeros((DEPTH, B, GLA_HEADS, GLA_DK, GLA_DV), dt)
    zeros_ffn = jnp.zeros((DEPTH, B, FFN_CONV - 1, 2 * D_FF), dt)
    meta_pad = (-N_META) % GLA_CHUNK
    yp, rg_conv_prompt, rg_h_prompt, gla_prompt, ffn_conv_prompt = _trunk(
        xp, zeros_rg_buf, zeros_h, zeros_S, zeros_ffn, meta_pad, p, final_norm_g)
    y_prompt = yp[:, N_META:]
    y_sample, rg_conv_sample, rg_h_sample, gla_sample, ffn_conv_sample = _trunk(
        x_sample, state_rg_conv, state_rg_h, state_gla, state_ffn_conv, 0, p, final_norm_g)
    return (y_prompt, y_sample, rg_conv_prompt, rg_h_prompt, gla_prompt, ffn_conv_prompt,
            rg_conv_sample, rg_h_sample, gla_sample, ffn_conv_sample)
```

```python
import functools

import jax
import jax.numpy as jnp
from jax import lax
from jax.experimental import pallas as pl
from jax.experimental.pallas import tpu as pltpu

F32 = jnp.float32
BF16 = jnp.bfloat16

D_MODEL = 1024
DEPTH = 4
D_RNN = D_MODEL
RG_BLOCKS = 16
RG_BW = D_RNN // RG_BLOCKS
RG_CONV = 4
RG_C = 8.0
GLA_HEADS = 4
GLA_DK = 128
GLA_DV = 256
QK_DIM = GLA_HEADS * GLA_DK
V_DIM = GLA_HEADS * GLA_DV
GLA_RANK = 16
GLA_TAU = 16.0
GLA_CHUNK = 64
D_FF = 3 * D_MODEL
FFN_CONV = 3
N_META = 16
EPS = 1e-6

V7X_MXU_DIM = 256
V7X_LANES = 128
V7X_SUBLANES = 8

W_A_COLS = 2 * D_RNN + 2 * QK_DIM + V_DIM
W_GLR_OFF = W_A_COLS
W_B_OFF = W_A_COLS + GLA_RANK
W_B_COLS = V_DIM + 2 * D_MODEL

SEQ_TILE = 256
FFN_COL_TILE = 1024
SAMPLE_ROWS = 8
SAMPLE_FF_TILE = 512
VMEM_LIMIT = 56 * 1024 * 1024


def _dot(a, b):
    return jnp.dot(a, b, preferred_element_type=F32)


def _rmsnorm(x, g):
    ms = jnp.mean(x * x, axis=-1, keepdims=True)
    return x * lax.rsqrt(ms + EPS) * g


def _softplus(x):
    return jnp.maximum(x, 0.0) + jnp.log1p(jnp.exp(-jnp.abs(x)))


def _block_diag_dot(xb, w_ref):
    n = D_RNN // V7X_MXU_DIM
    return jnp.concatenate(
        [_dot(xb[:, i * V7X_MXU_DIM:(i + 1) * V7X_MXU_DIM], w_ref[i]) for i in range(n)], axis=1)


def _rglru_coeffs(xc, rwa_ref, rba_ref, rwx_ref, rbx_ref, lam_ref):
    xcb = xc.astype(BF16)
    r = jax.nn.sigmoid(_block_diag_dot(xcb, rwa_ref) + rba_ref[...])
    i = jax.nn.sigmoid(_block_diag_dot(xcb, rwx_ref) + rbx_ref[...])
    log_a = (-RG_C) * r * _softplus(-lam_ref[...])
    a = jnp.exp(log_a)
    b = jnp.sqrt(1.0 - a * a) * (i * xc)
    return a, b


def _head_norm_gate(o, gng, g_out):
    parts = []
    for hh in range(GLA_HEADS):
        sl = slice(hh * GLA_DV, (hh + 1) * GLA_DV)
        oh = o[:, sl]
        ms = jnp.mean(oh * oh, axis=-1, keepdims=True)
        parts.append(oh * lax.rsqrt(ms + EPS) * gng[:, sl])
    on = jnp.concatenate(parts, axis=1)
    return on * (g_out * jax.nn.sigmoid(g_out))


def _mixer_seq_kernel(x_ref, rgbuf0_ref, h0_ref, s0_ref, g1_ref, wa_ref, wglr_ref, wb_ref,
                      cw_ref, cb_ref, rwa_ref, rba_ref, rwx_ref, rbx_ref, lam_ref,
                      wgate_ref, bgate_ref, gng_ref, wbra_ref, wbrb_ref, wout_ref,
                      xo_ref, rgbuf_ref, h_ref, s_ref,
                      xx_ref, o_ref, *, tt, chunk):
    hist = RG_CONV - 1

    @pl.when(pl.program_id(1) == 0)
    def _():
        rgbuf_ref[...] = rgbuf0_ref[...]
        h_ref[...] = h0_ref[...]
        s_ref[...] = s0_ref[...]

    x = x_ref[...]
    xn = _rmsnorm(x, g1_ref[...]).astype(BF16)
    za = _dot(xn, wa_ref[...])
    glr = _dot(xn, wglr_ref[...])
    zb = _dot(xn, wb_ref[...])

    rg_x = za[:, 0:D_RNN]
    base = V7X_SUBLANES - hist
    xx_ref[base:V7X_SUBLANES, :] = rgbuf_ref[...]
    xx_ref[V7X_SUBLANES:V7X_SUBLANES + tt, :] = rg_x
    cw = cw_ref[...]
    xc = cb_ref[...] + xx_ref[base:base + tt, :] * cw[0:1]
    for j in range(1, hist):
        xc = xc + xx_ref[base + j:base + j + tt, :] * cw[j:j + 1]
    xc = xc + rg_x * cw[hist:hist + 1]
    rgbuf_ref[...] = xx_ref[tt + base:tt + V7X_SUBLANES, :]

    a, b = _rglru_coeffs(xc, rwa_ref, rba_ref, rwx_ref, rbx_ref, lam_ref)
    row = lax.broadcasted_iota(jnp.int32, (tt, D_RNN), 0)
    s = 1
    while s < tt:
        m = row >= s
        b = jnp.where(m, a * pltpu.roll(b, s, axis=0) + b, b)
        a = jnp.where(m, a * pltpu.roll(a, s, axis=0), a)
        s *= 2
    h = b + a * h_ref[...]
    h_ref[...] = h[tt - 1:tt, :]
    ya = _dot((h * jax.nn.gelu(za[:, D_RNN:2 * D_RNN])).astype(BF16), wbra_ref[...])

    zg = _dot(glr.astype(BF16), wgate_ref[...]) + bgate_ref[...]
    la = -_softplus(-zg) * (1.0 / GLA_TAU)
    shift = chunk.bit_length() - 1
    r_i = lax.broadcasted_iota(jnp.int32, (tt, tt), 0)
    c_i = lax.broadcasted_iota(jnp.int32, (tt, tt), 1)
    same = (r_i >> shift) == (c_i >> shift)
    cum = jnp.where(same, jnp.where(c_i <= r_i, 1.0, 0.0), 0.0).astype(BF16)
    la_hi = la.astype(BF16)
    la_lo = (la - la_hi.astype(F32)).astype(BF16)
    bc = _dot(cum, la_hi) + _dot(cum, la_lo)
    tri = (lax.broadcasted_iota(jnp.int32, (chunk, chunk), 0)
           >= lax.broadcasted_iota(jnp.int32, (chunk, chunk), 1))
    q_off, k_off, v_off = 2 * D_RNN, 2 * D_RNN + QK_DIM, 2 * D_RNN + 2 * QK_DIM
    for c in range(tt // chunk):
        rs = slice(c * chunk, (c + 1) * chunk)
        for hh in range(GLA_HEADS):
            ks = slice(hh * GLA_DK, (hh + 1) * GLA_DK)
            g = bc[rs, ks]
            gl = g[chunk - 1:chunk, :]
            qh = za[rs, q_off + hh * GLA_DK:q_off + (hh + 1) * GLA_DK] * (GLA_DK ** -0.5)
            kh = za[rs, k_off + hh * GLA_DK:k_off + (hh + 1) * GLA_DK]
            vh = za[rs, v_off + hh * GLA_DV:v_off + (hh + 1) * GLA_DV].astype(BF16)
            qd = (qh * jnp.exp(g)).astype(BF16)
            kd = (kh * jnp.exp(-g)).astype(BF16)
            ke = (kh * jnp.exp(gl - g)).astype(BF16)
            att = lax.dot_general(qd, kd, (((1,), (1,)), ((), ())), preferred_element_type=F32)
            att = jnp.where(tri, att, 0.0).astype(BF16)
            st = s_ref[hh]
            o_ref[rs, hh * GLA_DV:(hh + 1) * GLA_DV] = _dot(att, vh) + _dot(qd, st.astype(BF16))
            ds = lax.dot_general(ke, vh, (((0,), (0,)), ((), ())), preferred_element_type=F32)
            dec = jnp.broadcast_to(jnp.exp(gl), (GLA_DK, GLA_DK)).T
            s_ref[hh] = jnp.concatenate([dec] * (GLA_DV // GLA_DK), axis=1) * st + ds

    on = _head_norm_gate(o_ref[...], gng_ref[...], zb[:, 0:V_DIM])
    yb = _dot(on.astype(BF16), wbrb_ref[...])
    merged = (jax.nn.sigmoid(zb[:, V_DIM:V_DIM + D_MODEL]) * ya
              + jax.nn.sigmoid(zb[:, V_DIM + D_MODEL:V_DIM + 2 * D_MODEL]) * yb)
    xo_ref[...] = x + _dot(merged.astype(BF16), wout_ref[...])


def _ffn_seq_kernel(x_ref, buf0_ref, g2_ref, wup_ref, cw_ref, cb_ref, wdn_ref, gf_ref,
                    xo_ref, buf_ref, ug_ref, uv_ref, *, tt, cf, final):
    hist = FFN_CONV - 1
    base = V7X_SUBLANES - hist

    @pl.when(pl.program_id(1) == 0)
    def _():
        buf_ref[...] = buf0_ref[...]

    x = x_ref[...]
    xn = _rmsnorm(x, g2_ref[...]).astype(BF16)
    acc = x
    for j in range(D_FF // cf):
        halves = []
        for off, u_ref in ((j * cf, ug_ref), (D_FF + j * cf, uv_ref)):
            cs = slice(off, off + cf)
            u = _dot(xn, wup_ref[:, cs])
            u_ref[base:V7X_SUBLANES, :] = buf_ref[:, cs]
            u_ref[V7X_SUBLANES:V7X_SUBLANES + tt, :] = u
            cw = cw_ref[:, cs]
            uc = cb_ref[:, cs] + u_ref[base:base + tt, :] * cw[0:1]
            for jj in range(1, hist):
                uc = uc + u_ref[base + jj:base + jj + tt, :] * cw[jj:jj + 1]
            uc = uc + u * cw[hist:hist + 1]
            buf_ref[:, cs] = u_ref[tt + base:tt + V7X_SUBLANES, :]
            halves.append(uc)
        hmid = (jax.nn.gelu(halves[0]) * halves[1]).astype(BF16)
        acc = acc + _dot(hmid, wdn_ref[j * cf:(j + 1) * cf, :])
    xo_ref[...] = _rmsnorm(acc, gf_ref[...]) if final else acc


def _layer_spec(tail, l, single=True):
    zeros = (0,) * len(tail)
    kw = dict(pipeline_mode=pl.Buffered(1)) if single else {}
    return pl.BlockSpec((None,) + tuple(tail), lambda *_: (l,) + zeros, **kw)


def _mixer_seq(x, rgbuf0, h0, s0, w, l, tt, chunk):
    bsz, t_len, _ = x.shape
    bi = 1 if rgbuf0.shape[0] == bsz else 0

    def state_spec(tail):
        zeros = (0,) * len(tail)
        return pl.BlockSpec((None,) + tuple(tail), lambda b, t: (b * bi,) + zeros)

    def out_state_spec(tail):
        zeros = (0,) * len(tail)
        return pl.BlockSpec((None,) + tuple(tail), lambda b, t: (b,) + zeros)

    x_spec = pl.BlockSpec((None, tt, D_MODEL), lambda b, t: (b, t, 0))
    hist = RG_CONV - 1
    names = ("norm1_g", "w_a", "w_glr", "w_b", "rg_conv_w", "rg_conv_b", "rg_wa", "rg_ba",
             "rg_wx", "rg_bx", "rg_lambda", "gla_w_gate", "gla_b_gate", "gla_norm_g",
             "w_branch_a", "w_branch_b", "w_out")
    return pl.pallas_call(
        functools.partial(_mixer_seq_kernel, tt=tt, chunk=chunk),
        grid=(bsz, t_len // tt),
        in_specs=[x_spec, state_spec((hist, D_RNN)), state_spec((1, D_RNN)),
                  state_spec((GLA_HEADS, GLA_DK, GLA_DV))]
                 + [_layer_spec(w[n].shape[1:], l) for n in names],
        out_specs=[x_spec, out_state_spec((hist, D_RNN)), out_state_spec((1, D_RNN)),
                   out_state_spec((GLA_HEADS, GLA_DK, GLA_DV))],
        out_shape=[jax.ShapeDtypeStruct(x.shape, F32),
                   jax.ShapeDtypeStruct((bsz, hist, D_RNN), F32),
                   jax.ShapeDtypeStruct((bsz, 1, D_RNN), F32),
                   jax.ShapeDtypeStruct((bsz, GLA_HEADS, GLA_DK, GLA_DV), F32)],
        scratch_shapes=[pltpu.VMEM((tt + V7X_SUBLANES, D_RNN), F32),
                        pltpu.VMEM((tt, V_DIM), F32)],
        compiler_params=pltpu.CompilerParams(
            dimension_semantics=("arbitrary", "arbitrary"), vmem_limit_bytes=VMEM_LIMIT),
        name="mixer_seq",
    )(x, rgbuf0, h0, s0, *[w[n] for n in names])


def _ffn_seq(x, buf0, w, l, tt, final):
    bsz, t_len, _ = x.shape
    bi = 1 if buf0.shape[0] == bsz else 0
    hist = FFN_CONV - 1
    cf = FFN_COL_TILE
    x_spec = pl.BlockSpec((None, tt, D_MODEL), lambda b, t: (b, t, 0))
    names = ("norm2_g", "w_ffn_up", "ffn_conv_w", "ffn_conv_b", "w_ffn_down")
    return pl.pallas_call(
        functools.partial(_ffn_seq_kernel, tt=tt, cf=cf, final=final),
        grid=(bsz, t_len // tt),
        in_specs=[x_spec, pl.BlockSpec((None, hist, 2 * D_FF), lambda b, t: (b * bi, 0, 0))]
                 + [_layer_spec(w[n].shape[1:], l) for n in names]
                 + [pl.BlockSpec((1, D_MODEL), lambda b, t: (0, 0))],
        out_specs=[x_spec, pl.BlockSpec((None, hist, 2 * D_FF), lambda b, t: (b, 0, 0))],
        out_shape=[jax.ShapeDtypeStruct(x.shape, F32),
                   jax.ShapeDtypeStruct((bsz, hist, 2 * D_FF), F32)],
        scratch_shapes=[pltpu.VMEM((tt + V7X_SUBLANES, cf), F32),
                        pltpu.VMEM((tt + V7X_SUBLANES, cf), F32)],
        compiler_params=pltpu.CompilerParams(
            dimension_semantics=("arbitrary", "arbitrary"), vmem_limit_bytes=VMEM_LIMIT),
        name="ffn_seq",
    )(x, buf0, *[w[n] for n in names], w["final_norm_g"])


def _trunk_seq(x, states, w, tt, chunk):
    outs = []
    for l in range(DEPTH):
        rgbuf0, h0, s0, fbuf0 = states[l]
        x, rgbuf, h, s = _mixer_seq(x, rgbuf0, h0, s0, w, l, tt, chunk)
        x, fbuf = _ffn_seq(x, fbuf0, w, l, tt, final=(l == DEPTH - 1))
        outs.append((rgbuf, h, s, fbuf))
    return x, outs


def _sample_pre_kernel(x_ref, rgbuf_ref, h0_ref, g1_ref, wa_ref, wglr_ref, wb_ref,
                       cw_ref, cb_ref, rwa_ref, rba_ref, rwx_ref, rbx_ref, lam_ref,
                       wgate_ref, bgate_ref, wbra_ref,
                       rgx_ref, h_ref, ya_ref, at_ref, kt_ref, qt_ref, v_ref, zb_ref):
    hist = RG_CONV - 1
    xn = _rmsnorm(x_ref[...], g1_ref[...]).astype(BF16)
    za = _dot(xn, wa_ref[...])
    glr = _dot(xn, wglr_ref[...])
    zb_ref[...] = _dot(xn, wb_ref[...])

    rg_x = za[:, 0:D_RNN]
    rgx_ref[...] = rg_x
    cw = cw_ref[...]
    xc = cb_ref[...] + rgbuf_ref[:, 0:D_RNN] * cw[0:1]
    for j in range(1, hist):
        xc = xc + rgbuf_ref[:, j * D_RNN:(j + 1) * D_RNN] * cw[j:j + 1]
    xc = xc + rg_x * cw[hist:hist + 1]
    a, b = _rglru_coeffs(xc, rwa_ref, rba_ref, rwx_ref, rbx_ref, lam_ref)
    h = a * h0_ref[...] + b
    h_ref[...] = h
    ya_ref[...] = _dot((h * jax.nn.gelu(za[:, D_RNN:2 * D_RNN])).astype(BF16), wbra_ref[...])

    zg = _dot(glr.astype(BF16), wgate_ref[...]) + bgate_ref[...]
    alpha = jnp.exp(-_softplus(-zg) * (1.0 / GLA_TAU))
    q_off, k_off, v_off = 2 * D_RNN, 2 * D_RNN + QK_DIM, 2 * D_RNN + 2 * QK_DIM
    at_ref[...] = alpha.T
    kt_ref[...] = za[:, k_off:k_off + QK_DIM].T
    qt_ref[...] = (za[:, q_off:q_off + QK_DIM] * (GLA_DK ** -0.5)).T
    v_ref[...] = za[:, v_off:v_off + V_DIM]


def _sample_state_kernel(s0_ref, at_ref, kt_ref, qt_ref, v_ref, s_ref, o_ref, *, nb):
    i = pl.program_id(0)
    al = at_ref[...]
    a_hi = al.astype(BF16)
    r1 = al - a_hi.astype(F32)
    a_mid = r1.astype(BF16)
    a_lo = (r1 - a_mid.astype(F32)).astype(BF16)
    kb = kt_ref[...].astype(BF16)
    qb = qt_ref[...].astype(BF16)
    rows = lax.broadcasted_iota(jnp.int32, (nb, GLA_DV), 0)
    for j in range(SAMPLE_ROWS):
        e = jnp.where(rows == i * SAMPLE_ROWS + j, 1.0, 0.0).astype(BF16)
        ca = _dot(a_hi, e) + _dot(a_mid, e) + _dot(a_lo, e)
        ck = _dot(kb, e)
        cq = _dot(qb, e)
        for hh in range(GLA_HEADS):
            ks = slice(hh * GLA_DK, (hh + 1) * GLA_DK)
            vs = slice(hh * GLA_DV, (hh + 1) * GLA_DV)
            sn = ca[ks] * s0_ref[j, hh] + ck[ks] * v_ref[j:j + 1, vs]
            s_ref[j, hh] = sn
            o_ref[j:j + 1, vs] = jnp.sum(cq[ks] * sn, axis=0, keepdims=True)


def _sample_post_kernel(x_ref, ya_ref, o_ref, zb_ref, gng_ref, wbrb_ref, wout_ref, g2_ref,
                        wupg_ref, wupv_ref, cwg_ref, cwv_ref, cbg_ref, cbv_ref,
                        bg0_ref, bv0_ref, bg1_ref, bv1_ref, wdn_ref, gf_ref,
                        xo_ref, ug_ref, uv_ref, xn_ref, *, final):
    j = pl.program_id(0)

    @pl.when(j == 0)
    def _():
        zb = zb_ref[...]
        on = _head_norm_gate(o_ref[...], gng_ref[...], zb[:, 0:V_DIM])
        yb = _dot(on.astype(BF16), wbrb_ref[...])
        merged = (jax.nn.sigmoid(zb[:, V_DIM:V_DIM + D_MODEL]) * ya_ref[...]
                  + jax.nn.sigmoid(zb[:, V_DIM + D_MODEL:V_DIM + 2 * D_MODEL]) * yb)
        x1 = x_ref[...] + _dot(merged.astype(BF16), wout_ref[...])
        xo_ref[...] = x1
        xn_ref[...] = _rmsnorm(x1, g2_ref[...]).astype(BF16)

    xn = xn_ref[...]
    ug = _dot(xn, wupg_ref[...])
    uv = _dot(xn, wupv_ref[...])
    ug_ref[...] = ug
    uv_ref[...] = uv
    cwg = cwg_ref[...]
    cwv = cwv_ref[...]
    gate = cbg_ref[...] + bg0_ref[...] * cwg[0:1] + bg1_ref[...] * cwg[1:2] + ug * cwg[2:3]
    val = cbv_ref[...] + bv0_ref[...] * cwv[0:1] + bv1_ref[...] * cwv[1:2] + uv * cwv[2:3]
    xo_ref[...] += _dot((jax.nn.gelu(gate) * val).astype(BF16), wdn_ref[...])

    if final:
        @pl.when(j == pl.num_programs(0) - 1)
        def _():
            xo_ref[...] = _rmsnorm(xo_ref[...], gf_ref[...])


def _sample_layer(x, rgbuf, h0, s0, fbuf, w, l, final):
    nb = x.shape[0]
    hist = RG_CONV - 1

    def full(shape):
        zeros = (0,) * len(shape)
        return pl.BlockSpec(tuple(shape), lambda *_: zeros)

    pre_names = ("norm1_g", "w_a", "w_glr", "w_b", "rg_conv_w", "rg_conv_b", "rg_wa", "rg_ba",
                 "rg_wx", "rg_bx", "rg_lambda", "gla_w_gate", "gla_b_gate", "w_branch_a")
    pre_out = [(nb, D_RNN), (nb, D_RNN), (nb, D_MODEL), (QK_DIM, nb), (QK_DIM, nb), (QK_DIM, nb),
               (nb, V_DIM), (nb, W_B_COLS)]
    rg_x, h, ya, a_t, k_t, q_t, v, zb = pl.pallas_call(
        _sample_pre_kernel,
        grid=(1,),
        in_specs=[full(x.shape), full(rgbuf.shape), full(h0.shape)]
                 + [_layer_spec(w[n].shape[1:], l) for n in pre_names],
        out_specs=[full(s) for s in pre_out],
        out_shape=[jax.ShapeDtypeStruct(s, F32) for s in pre_out],
        compiler_params=pltpu.CompilerParams(
            dimension_semantics=("arbitrary",), vmem_limit_bytes=VMEM_LIMIT),
        name="sample_pre",
    )(x, rgbuf, h0, *[w[n] for n in pre_names])

    s_blk = (SAMPLE_ROWS, GLA_HEADS, GLA_DK, GLA_DV)
    s_spec = pl.BlockSpec(s_blk, lambda i: (i, 0, 0, 0))
    row_spec = pl.BlockSpec((SAMPLE_ROWS, V_DIM), lambda i: (i, 0))
    s_new, o = pl.pallas_call(
        functools.partial(_sample_state_kernel, nb=nb),
        grid=(nb // SAMPLE_ROWS,),
        in_specs=[s_spec, full((QK_DIM, nb)), full((QK_DIM, nb)), full((QK_DIM, nb)), row_spec],
        out_specs=[s_spec, row_spec],
        out_shape=[jax.ShapeDtypeStruct(s0.shape, F32), jax.ShapeDtypeStruct((nb, V_DIM), F32)],
        compiler_params=pltpu.CompilerParams(
            dimension_semantics=("arbitrary",), vmem_limit_bytes=VMEM_LIMIT),
        name="sample_state",
    )(s0, a_t, k_t, q_t, v)

    tf = SAMPLE_FF_TILE
    nf = D_FF // tf

    def lcol(rows, off):
        return pl.BlockSpec((None, rows, tf), lambda j: (l, 0, off + j))

    def bcol(off):
        return pl.BlockSpec((nb, tf), lambda j: (0, off + j))

    xo, ug, uv = pl.pallas_call(
        functools.partial(_sample_post_kernel, final=final),
        grid=(nf,),
        in_specs=[full(x.shape), full(ya.shape), full(o.shape), full(zb.shape),
                  _layer_spec(w["gla_norm_g"].shape[1:], l, single=False),
                  _layer_spec(w["w_branch_b"].shape[1:], l, single=False),
                  _layer_spec(w["w_out"].shape[1:], l, single=False),
                  _layer_spec(w["norm2_g"].shape[1:], l, single=False),
                  lcol(D_MODEL, 0), lcol(D_MODEL, nf),
                  lcol(FFN_CONV, 0), lcol(FFN_CONV, nf),
                  lcol(1, 0), lcol(1, nf),
                  bcol(0), bcol(nf), bcol(2 * nf), bcol(3 * nf),
                  pl.BlockSpec((None, tf, D_MODEL), lambda j: (l, j, 0)),
                  full((1, D_MODEL))],
        out_specs=[full(x.shape), bcol(0), bcol(0)],
        out_shape=[jax.ShapeDtypeStruct(x.shape, F32),
                   jax.ShapeDtypeStruct((nb, D_FF), F32), jax.ShapeDtypeStruct((nb, D_FF), F32)],
        scratch_shapes=[pltpu.VMEM((nb, D_MODEL), BF16)],
        compiler_params=pltpu.CompilerParams(
            dimension_semantics=("arbitrary",), vmem_limit_bytes=VMEM_LIMIT),
        name="sample_post",
    )(x, ya, o, zb, w["gla_norm_g"], w["w_branch_b"], w["w_out"], w["norm2_g"],
      w["w_ffn_up"], w["w_ffn_up"], w["ffn_conv_w"], w["ffn_conv_w"],
      w["ffn_conv_b"], w["ffn_conv_b"], fbuf, fbuf, fbuf, fbuf, w["w_ffn_down"],
      w["final_norm_g"])
    return xo, rg_x, h, s_new, jnp.concatenate([ug, uv], axis=1)


def _prep_weights(norm1_g, w_in, rg_conv_w, rg_conv_b, rg_wa, rg_ba, rg_wx, rg_bx, rg_lambda,
                  gla_w_gate, gla_b_gate, gla_norm_g, w_branch_a, w_branch_b, w_out, norm2_g,
                  w_ffn_up, ffn_conv_w, ffn_conv_b, w_ffn_down, final_norm_g):
    def vec(a):
        return a.reshape(DEPTH, 1, -1).astype(F32)

    def block_diag(wb):
        per = V7X_MXU_DIM // RG_BW
        w5 = wb.reshape(DEPTH, RG_BLOCKS // per, per, RG_BW, RG_BW)
        bd = jnp.einsum("lgicd,ij->lgicjd", w5, jnp.eye(per, dtype=wb.dtype))
        return bd.reshape(DEPTH, RG_BLOCKS // per, V7X_MXU_DIM, V7X_MXU_DIM).astype(BF16)

    pad_r = V7X_LANES - GLA_RANK
    return dict(
        norm1_g=vec(norm1_g),
        w_a=w_in[:, :, 0:W_A_COLS].astype(BF16),
        w_glr=jnp.pad(w_in[:, :, W_GLR_OFF:W_B_OFF], ((0, 0), (0, 0), (0, pad_r))).astype(BF16),
        w_b=w_in[:, :, W_B_OFF:W_B_OFF + W_B_COLS].astype(BF16),
        rg_conv_w=rg_conv_w.astype(F32), rg_conv_b=vec(rg_conv_b),
        rg_wa=block_diag(rg_wa), rg_ba=vec(rg_ba), rg_wx=block_diag(rg_wx), rg_bx=vec(rg_bx),
        rg_lambda=vec(rg_lambda),
        gla_w_gate=jnp.pad(gla_w_gate, ((0, 0), (0, pad_r), (0, 0))).astype(BF16),
        gla_b_gate=vec(gla_b_gate), gla_norm_g=vec(gla_norm_g),
        w_branch_a=w_branch_a.astype(BF16), w_branch_b=w_branch_b.astype(BF16),
        w_out=w_out.astype(BF16), norm2_g=vec(norm2_g),
        w_ffn_up=w_ffn_up.astype(BF16), ffn_conv_w=ffn_conv_w.astype(F32),
        ffn_conv_b=vec(ffn_conv_b), w_ffn_down=w_ffn_down.astype(BF16),
        final_norm_g=final_norm_g.reshape(1, D_MODEL).astype(F32),
    )


def kernel(x_prompt, x_sample, state_rg_conv, state_rg_h, state_gla, state_ffn_conv, meta_tokens, norm1_g, w_in, rg_conv_w, rg_conv_b, rg_wa, rg_ba, rg_wx, rg_bx, rg_lambda, gla_w_gate, gla_b_gate, gla_norm_g, w_branch_a, w_branch_b, w_out, norm2_g, w_ffn_up, ffn_conv_w, ffn_conv_b, w_ffn_down, final_norm_g):
    w = _prep_weights(norm1_g, w_in, rg_conv_w, rg_conv_b, rg_wa, rg_ba, rg_wx, rg_bx, rg_lambda,
                      gla_w_gate, gla_b_gate, gla_norm_g, w_branch_a, w_branch_b, w_out, norm2_g,
                      w_ffn_up, ffn_conv_w, ffn_conv_b, w_ffn_down, final_norm_g)
    bsz = x_prompt.shape[0]

    zero_states = [(jnp.zeros((1, RG_CONV - 1, D_RNN), F32), jnp.zeros((1, 1, D_RNN), F32),
                    jnp.zeros((1, GLA_HEADS, GLA_DK, GLA_DV), F32),
                    jnp.zeros((1, FFN_CONV - 1, 2 * D_FF), F32))] * DEPTH
    _, meta_states = _trunk_seq(meta_tokens.astype(F32)[None], zero_states, w,
                                tt=N_META, chunk=N_META)

    y_prompt, p_states = _trunk_seq(x_prompt, meta_states, w, tt=SEQ_TILE, chunk=GLA_CHUNK)
    rg_conv_prompt = jnp.stack([s[0] for s in p_states])
    rg_h_prompt = jnp.stack([s[1].reshape(bsz, D_RNN) for s in p_states])
    gla_prompt = jnp.stack([s[2] for s in p_states])
    ffn_conv_prompt = jnp.stack([s[3] for s in p_states])

    nb = x_sample.shape[0]
    xs = x_sample.reshape(nb, D_MODEL)
    rg_bufs, hs, ss, fbufs = [], [], [], []
    for l in range(DEPTH):
        rgbuf = state_rg_conv[l].reshape(nb, (RG_CONV - 1) * D_RNN)
        fbuf = state_ffn_conv[l].reshape(nb, (FFN_CONV - 1) * 2 * D_FF)
        xs, rg_x, h, s_new, u = _sample_layer(xs, rgbuf, state_rg_h[l], state_gla[l], fbuf, w, l,
                                              final=(l == DEPTH - 1))
        rg_bufs.append(jnp.concatenate([state_rg_conv[l][:, 1:], rg_x[:, None]], axis=1))
        hs.append(h)
        ss.append(s_new)
        fbufs.append(jnp.concatenate([state_ffn_conv[l][:, 1:], u[:, None]], axis=1))
    y_sample = xs.reshape(x_sample.shape)
    return (y_prompt, y_sample, rg_conv_prompt, rg_h_prompt, gla_prompt, ffn_conv_prompt,
            jnp.stack(rg_bufs), jnp.stack(hs), jnp.stack(ss), jnp.stack(fbufs))
```

```python
import functools

import jax
import jax.numpy as jnp
from jax import lax
from jax.experimental import pallas as pl
from jax.experimental.pallas import tpu as pltpu

F32 = jnp.float32
BF16 = jnp.bfloat16

D_MODEL = 1024
DEPTH = 4
D_RNN = D_MODEL
RG_BLOCKS = 16
RG_BW = D_RNN // RG_BLOCKS
RG_CONV = 4
RG_C = 8.0
GLA_HEADS = 4
GLA_DK = 128
GLA_DV = 256
QK_DIM = GLA_HEADS * GLA_DK
V_DIM = GLA_HEADS * GLA_DV
GLA_RANK = 16
GLA_TAU = 16.0
GLA_CHUNK = 64
D_FF = 3 * D_MODEL
FFN_CONV = 3
N_META = 16
EPS = 1e-6

V7X_MXU_DIM = 256
V7X_LANES = 128
V7X_SUBLANES = 8

W_A_COLS = 2 * D_RNN + 2 * QK_DIM + V_DIM
W_GLR_OFF = W_A_COLS
W_B_OFF = W_A_COLS + GLA_RANK
W_B_COLS = V_DIM + 2 * D_MODEL
Q_OFF, K_OFF, V_OFF = 2 * D_RNN, 2 * D_RNN + QK_DIM, 2 * D_RNN + 2 * QK_DIM

SEQ_TILE = 512
FFN_COL_TILE = 1024
SAMPLE_ROWS = 8
SAMPLE_FF_TILE = 512
VMEM_LIMIT = 56 * 1024 * 1024


def _dot(a, b):
    return jnp.dot(a, b, preferred_element_type=F32)


def _rmsnorm(x, g):
    ms = jnp.mean(x * x, axis=-1, keepdims=True)
    return x * lax.rsqrt(ms + EPS) * g


def _softplus(x):
    return jnp.maximum(x, 0.0) + jnp.log1p(jnp.exp(-jnp.abs(x)))


def _block_diag_dot(xb, w_ref):
    n = D_RNN // V7X_MXU_DIM
    return jnp.concatenate(
        [_dot(xb[:, i * V7X_MXU_DIM:(i + 1) * V7X_MXU_DIM], w_ref[i]) for i in range(n)], axis=1)


def _rglru_coeffs(xc, rwa_ref, rba_ref, rwx_ref, rbx_ref, lam_ref):
    xcb = xc.astype(BF16)
    r = jax.nn.sigmoid(_block_diag_dot(xcb, rwa_ref) + rba_ref[...])
    i = jax.nn.sigmoid(_block_diag_dot(xcb, rwx_ref) + rbx_ref[...])
    log_a = (-RG_C) * r * _softplus(-lam_ref[...])
    a = jnp.exp(log_a)
    b = jnp.sqrt(1.0 - a * a) * (i * xc)
    return a, b


def _head_norm_gate(o, gng, g_out):
    parts = []
    for hh in range(GLA_HEADS):
        sl = slice(hh * GLA_DV, (hh + 1) * GLA_DV)
        oh = o[:, sl]
        ms = jnp.mean(oh * oh, axis=-1, keepdims=True)
        parts.append(oh * lax.rsqrt(ms + EPS) * gng[:, sl])
    on = jnp.concatenate(parts, axis=1)
    return on * (g_out * jax.nn.sigmoid(g_out))


def _delayed_rows(prev, u, s):
    ext = jnp.concatenate([prev, u], axis=0)
    return pltpu.roll(ext, s, axis=0)[V7X_SUBLANES:]


def _causal_conv(u, prev, cw, cb, hist):
    y = cb + _delayed_rows(prev, u, hist) * cw[0:1]
    for j in range(1, hist):
        y = y + _delayed_rows(prev, u, hist - j) * cw[j:j + 1]
    return y + u * cw[hist:hist + 1]


def _scan_affine(a, b, tt):
    width = a.shape[1]
    one = jnp.ones((V7X_SUBLANES, width), F32)
    zero = jnp.zeros((V7X_SUBLANES, width), F32)
    s = 1
    while s < min(V7X_SUBLANES, tt):
        b = a * _delayed_rows(zero, b, s) + b
        a = a * _delayed_rows(one, a, s)
        s *= 2
    while s < tt:
        b = jnp.concatenate([b[:s], a[s:] * b[:tt - s] + b[s:]], axis=0)
        a = jnp.concatenate([a[:s], a[s:] * a[:tt - s]], axis=0)
        s *= 2
    return a, b


def _mixer_seq_kernel(x_ref, rgbuf0_ref, h0_ref, s0_ref, g1_ref, wa_ref, wglr_ref, wb_ref,
                      cw_ref, cb_ref, rwa_ref, rba_ref, rwx_ref, rbx_ref, lam_ref,
                      wgate_ref, bgate_ref, gng_ref, wbra_ref, wbrb_ref, wout_ref,
                      xo_ref, rgbuf_ref, h_ref, s_ref,
                      hist_ref, o_ref, *, tt, chunk):
    hist = RG_CONV - 1

    @pl.when(pl.program_id(1) == 0)
    def _():
        hist_ref[...] = jnp.zeros(hist_ref.shape, F32)
        hist_ref[V7X_SUBLANES - hist:V7X_SUBLANES, :] = rgbuf0_ref[...]
        h_ref[...] = h0_ref[...]
        s_ref[...] = s0_ref[...]

    x = x_ref[...]
    xn = _rmsnorm(x, g1_ref[...]).astype(BF16)

    rg_x = _dot(xn, wa_ref[:, 0:D_RNN])
    xc = _causal_conv(rg_x, hist_ref[...], cw_ref[...], cb_ref[...], hist)
    hist_ref[...] = rg_x[tt - V7X_SUBLANES:tt, :]
    rgbuf_ref[...] = hist_ref[V7X_SUBLANES - hist:V7X_SUBLANES, :]
    a, b = _rglru_coeffs(xc, rwa_ref, rba_ref, rwx_ref, rbx_ref, lam_ref)

    glr = _dot(xn, wglr_ref[...])
    qk = _dot(xn, wa_ref[:, Q_OFF:V_OFF])
    rg_y = _dot(xn, wa_ref[:, D_RNN:2 * D_RNN])
    v = _dot(xn, wa_ref[:, V_OFF:V_OFF + V_DIM]).astype(BF16)
    a, b = _scan_affine(a, b, tt)
    h = b + a * h_ref[...]
    h_ref[...] = h[tt - 1:tt, :]
    ya = _dot((h * jax.nn.gelu(rg_y)).astype(BF16), wbra_ref[...])

    zg = _dot(glr.astype(BF16), wgate_ref[...]) + bgate_ref[...]
    la = -_softplus(-zg) * (1.0 / GLA_TAU)
    shift = chunk.bit_length() - 1
    r_i = lax.broadcasted_iota(jnp.int32, (tt, tt), 0)
    c_i = lax.broadcasted_iota(jnp.int32, (tt, tt), 1)
    causal = ((r_i >> shift) == (c_i >> shift)) & (c_i <= r_i)
    cum = jnp.where(causal, 1.0, 0.0).astype(BF16)
    la_hi = la.astype(BF16)
    la_lo = (la - la_hi.astype(F32)).astype(BF16)
    bc = _dot(cum, la_hi) + _dot(cum, la_lo)
    qd_all = (qk[:, 0:QK_DIM] * (GLA_DK ** -0.5) * jnp.exp(bc)).astype(BF16)
    kd_all = (qk[:, QK_DIM:2 * QK_DIM] * jnp.exp(-bc)).astype(BF16)
    zb = _dot(xn, wb_ref[...])
    for hh in range(GLA_HEADS):
        ks = slice(hh * GLA_DK, (hh + 1) * GLA_DK)
        vs = slice(hh * GLA_DV, (hh + 1) * GLA_DV)
        att = lax.dot_general(qd_all[:, ks], kd_all[:, ks], (((1,), (1,)), ((), ())),
                              preferred_element_type=F32)
        o_intra = _dot(jnp.where(causal, att, 0.0).astype(BF16), v[:, vs])
        for c in range(tt // chunk):
            rs = slice(c * chunk, (c + 1) * chunk)
            g = bc[rs, ks]
            gl = g[chunk - 1:chunk, :]
            ke = (qk[rs, QK_DIM + hh * GLA_DK:QK_DIM + (hh + 1) * GLA_DK]
                  * jnp.exp(gl - g)).astype(BF16)
            st = s_ref[hh]
            o_ref[rs, vs] = o_intra[rs] + _dot(qd_all[rs, ks], st.astype(BF16))
            ds = lax.dot_general(ke, v[rs, vs], (((0,), (0,)), ((), ())),
                                 preferred_element_type=F32)
            dec = jnp.broadcast_to(jnp.exp(gl), (GLA_DK, GLA_DK)).T
            s_ref[hh] = jnp.concatenate([dec] * (GLA_DV // GLA_DK), axis=1) * st + ds

    on = _head_norm_gate(o_ref[...], gng_ref[...], zb[:, 0:V_DIM])
    yb = _dot(on.astype(BF16), wbrb_ref[...])
    merged = (jax.nn.sigmoid(zb[:, V_DIM:V_DIM + D_MODEL]) * ya
              + jax.nn.sigmoid(zb[:, V_DIM + D_MODEL:V_DIM + 2 * D_MODEL]) * yb)
    xo_ref[...] = x + _dot(merged.astype(BF16), wout_ref[...])


def _ffn_seq_kernel(x_ref, buf0_ref, g2_ref, wup_ref, cw_ref, cb_ref, wdn_ref, gf_ref,
                    xo_ref, buf_ref, hist_ref, *, tt, cf, final):
    hist = FFN_CONV - 1

    @pl.when(pl.program_id(1) == 0)
    def _():
        hist_ref[...] = jnp.zeros(hist_ref.shape, F32)
        hist_ref[V7X_SUBLANES - hist:V7X_SUBLANES, :] = buf0_ref[...]

    x = x_ref[...]
    xn = _rmsnorm(x, g2_ref[...]).astype(BF16)
    def up_proj(j):
        return [(slice(off, off + cf), _dot(xn, wup_ref[:, off:off + cf]))
                for off in (j * cf, D_FF + j * cf)]

    acc = x
    n_chunks = D_FF // cf
    ups = up_proj(0)
    for j in range(n_chunks):
        nxt = up_proj(j + 1) if j + 1 < n_chunks else None
        halves = []
        for cs, u in ups:
            halves.append(_causal_conv(u, hist_ref[:, cs], cw_ref[:, cs], cb_ref[:, cs], hist))
            hist_ref[:, cs] = u[tt - V7X_SUBLANES:tt, :]
        hmid = (jax.nn.gelu(halves[0]) * halves[1]).astype(BF16)
        acc = acc + _dot(hmid, wdn_ref[j * cf:(j + 1) * cf, :])
        ups = nxt
    buf_ref[...] = hist_ref[V7X_SUBLANES - hist:V7X_SUBLANES, :]
    xo_ref[...] = _rmsnorm(acc, gf_ref[...]) if final else acc


def _layer_spec(tail, l, single=True):
    zeros = (0,) * len(tail)
    kw = dict(pipeline_mode=pl.Buffered(1)) if single else {}
    return pl.BlockSpec((None,) + tuple(tail), lambda *_: (l,) + zeros, **kw)


def _mixer_seq(x, rgbuf0, h0, s0, w, l, tt, chunk):
    bsz, t_len, _ = x.shape
    bi = 1 if rgbuf0.shape[0] == bsz else 0

    def state_spec(tail):
        zeros = (0,) * len(tail)
        return pl.BlockSpec((None,) + tuple(tail), lambda b, t: (b * bi,) + zeros)

    def out_state_spec(tail):
        zeros = (0,) * len(tail)
        return pl.BlockSpec((None,) + tuple(tail), lambda b, t: (b,) + zeros)

    x_spec = pl.BlockSpec((None, tt, D_MODEL), lambda b, t: (b, t, 0))
    hist = RG_CONV - 1
    names = ("norm1_g", "w_a", "w_glr", "w_b", "rg_conv_w", "rg_conv_b", "rg_wa", "rg_ba",
             "rg_wx", "rg_bx", "rg_lambda", "gla_w_gate", "gla_b_gate", "gla_norm_g",
             "w_branch_a", "w_branch_b", "w_out")
    return pl.pallas_call(
        functools.partial(_mixer_seq_kernel, tt=tt, chunk=chunk),
        grid=(bsz, t_len // tt),
        in_specs=[x_spec, state_spec((hist, D_RNN)), state_spec((1, D_RNN)),
                  state_spec((GLA_HEADS, GLA_DK, GLA_DV))]
                 + [_layer_spec(w[n].shape[1:], l) for n in names],
        out_specs=[x_spec, out_state_spec((hist, D_RNN)), out_state_spec((1, D_RNN)),
                   out_state_spec((GLA_HEADS, GLA_DK, GLA_DV))],
        out_shape=[jax.ShapeDtypeStruct(x.shape, F32),
                   jax.ShapeDtypeStruct((bsz, hist, D_RNN), F32),
                   jax.ShapeDtypeStruct((bsz, 1, D_RNN), F32),
                   jax.ShapeDtypeStruct((bsz, GLA_HEADS, GLA_DK, GLA_DV), F32)],
        scratch_shapes=[pltpu.VMEM((V7X_SUBLANES, D_RNN), F32),
                        pltpu.VMEM((tt, V_DIM), F32)],
        compiler_params=pltpu.CompilerParams(
            dimension_semantics=("arbitrary", "arbitrary"), vmem_limit_bytes=VMEM_LIMIT),
        name="mixer_seq",
    )(x, rgbuf0, h0, s0, *[w[n] for n in names])


def _ffn_seq(x, buf0, w, l, tt, final):
    bsz, t_len, _ = x.shape
    bi = 1 if buf0.shape[0] == bsz else 0
    hist = FFN_CONV - 1
    cf = FFN_COL_TILE
    x_spec = pl.BlockSpec((None, tt, D_MODEL), lambda b, t: (b, t, 0))
    names = ("norm2_g", "w_ffn_up", "ffn_conv_w", "ffn_conv_b", "w_ffn_down")
    return pl.pallas_call(
        functools.partial(_ffn_seq_kernel, tt=tt, cf=cf, final=final),
        grid=(bsz, t_len // tt),
        in_specs=[x_spec, pl.BlockSpec((None, hist, 2 * D_FF), lambda b, t: (b * bi, 0, 0))]
                 + [_layer_spec(w[n].shape[1:], l) for n in names]
                 + [pl.BlockSpec((1, D_MODEL), lambda b, t: (0, 0))],
        out_specs=[x_spec, pl.BlockSpec((None, hist, 2 * D_FF), lambda b, t: (b, 0, 0))],
        out_shape=[jax.ShapeDtypeStruct(x.shape, F32),
                   jax.ShapeDtypeStruct((bsz, hist, 2 * D_FF), F32)],
        scratch_shapes=[pltpu.VMEM((V7X_SUBLANES, 2 * D_FF), F32)],
        compiler_params=pltpu.CompilerParams(
            dimension_semantics=("arbitrary", "arbitrary"), vmem_limit_bytes=VMEM_LIMIT),
        name="ffn_seq",
    )(x, buf0, *[w[n] for n in names], w["final_norm_g"])


def _trunk_seq(x, states, w, tt, chunk):
    outs = []
    for l in range(DEPTH):
        rgbuf0, h0, s0, fbuf0 = states[l]
        x, rgbuf, h, s = _mixer_seq(x, rgbuf0, h0, s0, w, l, tt, chunk)
        x, fbuf = _ffn_seq(x, fbuf0, w, l, tt, final=(l == DEPTH - 1))
        outs.append((rgbuf, h, s, fbuf))
    return x, outs


def _sample_pre_kernel(x_ref, rgbuf_ref, h0_ref, g1_ref, wa_ref, wglr_ref, wb_ref,
                       cw_ref, cb_ref, rwa_ref, rba_ref, rwx_ref, rbx_ref, lam_ref,
                       wgate_ref, bgate_ref, wbra_ref,
                       rgx_ref, h_ref, ya_ref, at_ref, kt_ref, wt_ref, v_ref, okv_ref, zb_ref):
    hist = RG_CONV - 1
    xn = _rmsnorm(x_ref[...], g1_ref[...]).astype(BF16)
    za = _dot(xn, wa_ref[...])
    glr = _dot(xn, wglr_ref[...])
    zb_ref[...] = _dot(xn, wb_ref[...])

    rg_x = za[:, 0:D_RNN]
    rgx_ref[...] = rg_x
    cw = cw_ref[...]
    xc = cb_ref[...] + rgbuf_ref[:, 0:D_RNN] * cw[0:1]
    for j in range(1, hist):
        xc = xc + rgbuf_ref[:, j * D_RNN:(j + 1) * D_RNN] * cw[j:j + 1]
    xc = xc + rg_x * cw[hist:hist + 1]
    a, b = _rglru_coeffs(xc, rwa_ref, rba_ref, rwx_ref, rbx_ref, lam_ref)
    h = a * h0_ref[...] + b
    h_ref[...] = h
    ya_ref[...] = _dot((h * jax.nn.gelu(za[:, D_RNN:2 * D_RNN])).astype(BF16), wbra_ref[...])

    zg = _dot(glr.astype(BF16), wgate_ref[...]) + bgate_ref[...]
    alpha = jnp.exp(-_softplus(-zg) * (1.0 / GLA_TAU))
    q = za[:, Q_OFF:Q_OFF + QK_DIM] * (GLA_DK ** -0.5)
    k = za[:, K_OFF:K_OFF + QK_DIM]
    v = za[:, V_OFF:V_OFF + V_DIM]
    at_ref[...] = alpha.T
    kt_ref[...] = k.T
    wt_ref[...] = (q * alpha).T
    v_ref[...] = v
    qk = q * k
    okv_ref[...] = jnp.concatenate(
        [jnp.sum(qk[:, hh * GLA_DK:(hh + 1) * GLA_DK], axis=-1, keepdims=True)
         * v[:, hh * GLA_DV:(hh + 1) * GLA_DV] for hh in range(GLA_HEADS)], axis=1)


def _row_onehot(nb, row):
    rows = lax.broadcasted_iota(jnp.int32, (nb, GLA_DV), 0)
    return jnp.where(rows == row, 1.0, 0.0).astype(BF16)


def _sample_read_kernel(s0_ref, wt_ref, okv_ref, o_ref, *, nb):
    i = pl.program_id(0)
    wb = wt_ref[...].astype(BF16)
    for j in range(SAMPLE_ROWS):
        cw = _dot(wb, _row_onehot(nb, i * SAMPLE_ROWS + j))
        for hh in range(GLA_HEADS):
            ks = slice(hh * GLA_DK, (hh + 1) * GLA_DK)
            vs = slice(hh * GLA_DV, (hh + 1) * GLA_DV)
            o_ref[j:j + 1, vs] = (jnp.sum(cw[ks] * s0_ref[j, hh], axis=0, keepdims=True)
                                  + okv_ref[j:j + 1, vs])


def _sample_update_kernel(s0_ref, at_ref, kt_ref, v_ref, s_ref, *, nb):
    i = pl.program_id(1)
    al = at_ref[...]
    a_hi = al.astype(BF16)
    r1 = al - a_hi.astype(F32)
    a_mid = r1.astype(BF16)
    a_lo = (r1 - a_mid.astype(F32)).astype(BF16)
    kb = kt_ref[...].astype(BF16)
    for j in range(SAMPLE_ROWS):
        e = _row_onehot(nb, i * SAMPLE_ROWS + j)
        ca = _dot(a_hi, e) + _dot(a_mid, e) + _dot(a_lo, e)
        ck = _dot(kb, e)
        for hh in range(GLA_HEADS):
            ks = slice(hh * GLA_DK, (hh + 1) * GLA_DK)
            vs = slice(hh * GLA_DV, (hh + 1) * GLA_DV)
            s_ref[j, hh] = ca[ks] * s0_ref[j, hh] + ck[ks] * v_ref[j:j + 1, vs]


def _sample_post_kernel(x_ref, ya_ref, o_ref, zb_ref, gng_ref, wbrb_ref, wout_ref, g2_ref,
                        wupg_ref, wupv_ref, cwg_ref, cwv_ref, cbg_ref, cbv_ref,
                        bg0_ref, bv0_ref, bg1_ref, bv1_ref, wdn_ref, gf_ref,
                        xo_ref, ug_ref, uv_ref, xn_ref, *, final):
    j = pl.program_id(0)

    @pl.when(j == 0)
    def _():
        zb = zb_ref[...]
        on = _head_norm_gate(o_ref[...], gng_ref[...], zb[:, 0:V_DIM])
        yb = _dot(on.astype(BF16), wbrb_ref[...])
        merged = (jax.nn.sigmoid(zb[:, V_DIM:V_DIM + D_MODEL]) * ya_ref[...]
                  + jax.nn.sigmoid(zb[:, V_DIM + D_MODEL:V_DIM + 2 * D_MODEL]) * yb)
        x1 = x_ref[...] + _dot(merged.astype(BF16), wout_ref[...])
        xo_ref[...] = x1
        xn_ref[...] = _rmsnorm(x1, g2_ref[...]).astype(BF16)

    xn = xn_ref[...]
    ug = _dot(xn, wupg_ref[...])
    uv = _dot(xn, wupv_ref[...])
    ug_ref[...] = ug
    uv_ref[...] = uv
    cwg = cwg_ref[...]
    cwv = cwv_ref[...]
    gate = cbg_ref[...] + bg0_ref[...] * cwg[0:1] + bg1_ref[...] * cwg[1:2] + ug * cwg[2:3]
    val = cbv_ref[...] + bv0_ref[...] * cwv[0:1] + bv1_ref[...] * cwv[1:2] + uv * cwv[2:3]
    xo_ref[...] += _dot((jax.nn.gelu(gate) * val).astype(BF16), wdn_ref[...])

    if final:
        @pl.when(j == pl.num_programs(0) - 1)
        def _():
            xo_ref[...] = _rmsnorm(xo_ref[...], gf_ref[...])


def _full_spec(shape):
    zeros = (0,) * len(shape)
    return pl.BlockSpec(tuple(shape), lambda *_: zeros)


def _sample_layer(x, rgbuf_all, h_all, s_all, fbuf_all, w, l, final):
    nb = x.shape[0]
    params = pltpu.CompilerParams(dimension_semantics=("arbitrary",),
                                  vmem_limit_bytes=VMEM_LIMIT)

    pre_names = ("norm1_g", "w_a", "w_glr", "w_b", "rg_conv_w", "rg_conv_b", "rg_wa", "rg_ba",
                 "rg_wx", "rg_bx", "rg_lambda", "gla_w_gate", "gla_b_gate", "w_branch_a")
    pre_out = [(nb, D_RNN), (nb, D_RNN), (nb, D_MODEL), (QK_DIM, nb), (QK_DIM, nb), (QK_DIM, nb),
               (nb, V_DIM), (nb, V_DIM), (nb, W_B_COLS)]
    rg_x, h, ya, a_t, k_t, w_t, v, okv, zb = pl.pallas_call(
        _sample_pre_kernel,
        grid=(1,),
        in_specs=[_full_spec(x.shape), _layer_spec(rgbuf_all.shape[1:], l, single=False),
                  _layer_spec(h_all.shape[1:], l, single=False)]
                 + [_layer_spec(w[n].shape[1:], l) for n in pre_names],
        out_specs=[_full_spec(s) for s in pre_out],
        out_shape=[jax.ShapeDtypeStruct(s, F32) for s in pre_out],
        compiler_params=params,
        name="sample_pre",
    )(x, rgbuf_all, h_all, *[w[n] for n in pre_names])

    s_blk = (None, SAMPLE_ROWS, GLA_HEADS, GLA_DK, GLA_DV)
    row_spec = pl.BlockSpec((SAMPLE_ROWS, V_DIM), lambda i: (i, 0))
    o = pl.pallas_call(
        functools.partial(_sample_read_kernel, nb=nb),
        grid=(nb // SAMPLE_ROWS,),
        in_specs=[pl.BlockSpec(s_blk, lambda i: (l, i, 0, 0, 0)), _full_spec((QK_DIM, nb)),
                  row_spec],
        out_specs=row_spec,
        out_shape=jax.ShapeDtypeStruct((nb, V_DIM), F32),
        compiler_params=params,
        name="sample_read",
    )(s_all, w_t, okv)

    tf = SAMPLE_FF_TILE
    nf = D_FF // tf

    def lcol(rows, off):
        return pl.BlockSpec((None, rows, tf), lambda j: (l, 0, off + j))

    ucol = pl.BlockSpec((nb, tf), lambda j: (0, j))
    xo, ug, uv = pl.pallas_call(
        functools.partial(_sample_post_kernel, final=final),
        grid=(nf,),
        in_specs=[_full_spec(x.shape), _full_spec(ya.shape), _full_spec(o.shape),
                  _full_spec(zb.shape),
                  _layer_spec(w["gla_norm_g"].shape[1:], l, single=False),
                  _layer_spec(w["w_branch_b"].shape[1:], l, single=False),
                  _layer_spec(w["w_out"].shape[1:], l, single=False),
                  _layer_spec(w["norm2_g"].shape[1:], l, single=False),
                  lcol(D_MODEL, 0), lcol(D_MODEL, nf),
                  lcol(FFN_CONV, 0), lcol(FFN_CONV, nf),
                  lcol(1, 0), lcol(1, nf),
                  lcol(nb, 0), lcol(nb, nf), lcol(nb, 2 * nf), lcol(nb, 3 * nf),
                  pl.BlockSpec((None, tf, D_MODEL), lambda j: (l, j, 0)),
                  _full_spec((1, D_MODEL))],
        out_specs=[_full_spec(x.shape), ucol, ucol],
        out_shape=[jax.ShapeDtypeStruct(x.shape, F32),
                   jax.ShapeDtypeStruct((nb, D_FF), F32), jax.ShapeDtypeStruct((nb, D_FF), F32)],
        scratch_shapes=[pltpu.VMEM((nb, D_MODEL), BF16)],
        compiler_params=params,
        name="sample_post",
    )(x, ya, o, zb, w["gla_norm_g"], w["w_branch_b"], w["w_out"], w["norm2_g"],
      w["w_ffn_up"], w["w_ffn_up"], w["ffn_conv_w"], w["ffn_conv_w"],
      w["ffn_conv_b"], w["ffn_conv_b"], fbuf_all, fbuf_all, fbuf_all, fbuf_all,
      w["w_ffn_down"], w["final_norm_g"])
    return xo, rg_x, h, (a_t, k_t, v), (ug, uv)


def _sample_state_update(s_all, a_t, k_t, v):
    nb = v.shape[1]
    s_blk = (None, SAMPLE_ROWS, GLA_HEADS, GLA_DK, GLA_DV)
    s_spec = pl.BlockSpec(s_blk, lambda l, i: (l, i, 0, 0, 0))
    t_spec = pl.BlockSpec((None, QK_DIM, nb), lambda l, i: (l, 0, 0))
    return pl.pallas_call(
        functools.partial(_sample_update_kernel, nb=nb),
        grid=(DEPTH, nb // SAMPLE_ROWS),
        in_specs=[s_spec, t_spec, t_spec,
                  pl.BlockSpec((None, SAMPLE_ROWS, V_DIM), lambda l, i: (l, i, 0))],
        out_specs=s_spec,
        out_shape=jax.ShapeDtypeStruct(s_all.shape, F32),
        compiler_params=pltpu.CompilerParams(
            dimension_semantics=("arbitrary", "arbitrary"), vmem_limit_bytes=VMEM_LIMIT),
        name="sample_update",
    )(s_all, a_t, k_t, v)


def _prep_weights(norm1_g, w_in, rg_conv_w, rg_conv_b, rg_wa, rg_ba, rg_wx, rg_bx, rg_lambda,
                  gla_w_gate, gla_b_gate, gla_norm_g, w_branch_a, w_branch_b, w_out, norm2_g,
                  w_ffn_up, ffn_conv_w, ffn_conv_b, w_ffn_down, final_norm_g):
    def vec(a):
        return a.reshape(DEPTH, 1, -1).astype(F32)

    def block_diag(wb):
        per = V7X_MXU_DIM // RG_BW
        w5 = wb.reshape(DEPTH, RG_BLOCKS // per, per, RG_BW, RG_BW)
        bd = jnp.einsum("lgicd,ij->lgicjd", w5, jnp.eye(per, dtype=wb.dtype))
        return bd.reshape(DEPTH, RG_BLOCKS // per, V7X_MXU_DIM, V7X_MXU_DIM).astype(BF16)

    pad_r = V7X_LANES - GLA_RANK
    return dict(
        norm1_g=vec(norm1_g),
        w_a=w_in[:, :, 0:W_A_COLS].astype(BF16),
        w_glr=jnp.pad(w_in[:, :, W_GLR_OFF:W_B_OFF], ((0, 0), (0, 0), (0, pad_r))).astype(BF16),
        w_b=w_in[:, :, W_B_OFF:W_B_OFF + W_B_COLS].astype(BF16),
        rg_conv_w=rg_conv_w.astype(F32), rg_conv_b=vec(rg_conv_b),
        rg_wa=block_diag(rg_wa), rg_ba=vec(rg_ba), rg_wx=block_diag(rg_wx), rg_bx=vec(rg_bx),
        rg_lambda=vec(rg_lambda),
        gla_w_gate=jnp.pad(gla_w_gate, ((0, 0), (0, pad_r), (0, 0))).astype(BF16),
        gla_b_gate=vec(gla_b_gate), gla_norm_g=vec(gla_norm_g),
        w_branch_a=w_branch_a.astype(BF16), w_branch_b=w_branch_b.astype(BF16),
        w_out=w_out.astype(BF16), norm2_g=vec(norm2_g),
        w_ffn_up=w_ffn_up.astype(BF16), ffn_conv_w=ffn_conv_w.astype(F32),
        ffn_conv_b=vec(ffn_conv_b), w_ffn_down=w_ffn_down.astype(BF16),
        final_norm_g=final_norm_g.reshape(1, D_MODEL).astype(F32),
    )


def kernel(x_prompt, x_sample, state_rg_conv, state_rg_h, state_gla, state_ffn_conv, meta_tokens, norm1_g, w_in, rg_conv_w, rg_conv_b, rg_wa, rg_ba, rg_wx, rg_bx, rg_lambda, gla_w_gate, gla_b_gate, gla_norm_g, w_branch_a, w_branch_b, w_out, norm2_g, w_ffn_up, ffn_conv_w, ffn_conv_b, w_ffn_down, final_norm_g):
    w = _prep_weights(norm1_g, w_in, rg_conv_w, rg_conv_b, rg_wa, rg_ba, rg_wx, rg_bx, rg_lambda,
                      gla_w_gate, gla_b_gate, gla_norm_g, w_branch_a, w_branch_b, w_out, norm2_g,
                      w_ffn_up, ffn_conv_w, ffn_conv_b, w_ffn_down, final_norm_g)
    bsz = x_prompt.shape[0]

    zero_states = [(jnp.zeros((1, RG_CONV - 1, D_RNN), F32), jnp.zeros((1, 1, D_RNN), F32),
                    jnp.zeros((1, GLA_HEADS, GLA_DK, GLA_DV), F32),
                    jnp.zeros((1, FFN_CONV - 1, 2 * D_FF), F32))] * DEPTH
    _, meta_states = _trunk_seq(meta_tokens.astype(F32)[None], zero_states, w,
                                tt=N_META, chunk=N_META)

    y_prompt, p_states = _trunk_seq(x_prompt, meta_states, w, tt=SEQ_TILE, chunk=GLA_CHUNK)
    rg_conv_prompt = jnp.stack([s[0] for s in p_states])
    rg_h_prompt = jnp.stack([s[1].reshape(bsz, D_RNN) for s in p_states])
    gla_prompt = jnp.stack([s[2] for s in p_states])
    ffn_conv_prompt = jnp.stack([s[3] for s in p_states])

    nb = x_sample.shape[0]
    xs = x_sample.reshape(nb, D_MODEL)
    rgbuf_all = state_rg_conv.reshape(DEPTH, nb, (RG_CONV - 1) * D_RNN)
    fbuf_all = state_ffn_conv.reshape(DEPTH, nb, (FFN_CONV - 1) * 2 * D_FF)
    rg_xs, hs, akv, us = [], [], [], []
    for l in range(DEPTH):
        xs, rg_x, h, akv_l, u_l = _sample_layer(xs, rgbuf_all, state_rg_h, state_gla, fbuf_all,
                                                w, l, final=(l == DEPTH - 1))
        rg_xs.append(rg_x)
        hs.append(h)
        akv.append(akv_l)
        us.append(jnp.concatenate(u_l, axis=1))
    gla_sample = _sample_state_update(state_gla, *[jnp.stack(t) for t in zip(*akv)])
    rg_conv_sample = jnp.concatenate(
        [state_rg_conv[:, :, 1:], jnp.stack(rg_xs)[:, :, None]], axis=2)
    ffn_conv_sample = jnp.concatenate(
        [state_ffn_conv[:, :, 1:], jnp.stack(us)[:, :, None]], axis=2)
    y_sample = xs.reshape(x_sample.shape)
    return (y_prompt, y_sample, rg_conv_prompt, rg_h_prompt, gla_prompt, ffn_conv_prompt,
            rg_conv_sample, jnp.stack(hs), gla_sample, ffn_conv_sample)
```

```python
import functools

import jax
import jax.numpy as jnp
from jax import lax
from jax.experimental import pallas as pl
from jax.experimental.pallas import tpu as pltpu

F32 = jnp.float32
BF16 = jnp.bfloat16

D_MODEL = 1024
DEPTH = 4
D_RNN = D_MODEL
RG_BLOCKS = 16
RG_BW = D_RNN // RG_BLOCKS
RG_CONV = 4
RG_C = 8.0
GLA_HEADS = 4
GLA_DK = 128
GLA_DV = 256
QK_DIM = GLA_HEADS * GLA_DK
V_DIM = GLA_HEADS * GLA_DV
GLA_RANK = 16
GLA_TAU = 16.0
GLA_CHUNK = 64
D_FF = 3 * D_MODEL
FFN_CONV = 3
N_META = 16
EPS = 1e-6

V7X_MXU_DIM = 256
V7X_LANES = 128
V7X_SUBLANES = 8

W_A_COLS = 2 * D_RNN + 2 * QK_DIM + V_DIM
W_GLR_OFF = W_A_COLS
W_B_OFF = W_A_COLS + GLA_RANK
W_B_COLS = V_DIM + 2 * D_MODEL
Q_OFF, K_OFF, V_OFF = 2 * D_RNN, 2 * D_RNN + QK_DIM, 2 * D_RNN + 2 * QK_DIM

SEQ_TILE = 512
FFN_COL_TILE = 1024
SAMPLE_ROWS = 8
SAMPLE_FF_TILE = 512
VMEM_LIMIT = 56 * 1024 * 1024


def _dot(a, b):
    return jnp.dot(a, b, preferred_element_type=F32)


def _rmsnorm(x, g):
    ms = jnp.mean(x * x, axis=-1, keepdims=True)
    return x * lax.rsqrt(ms + EPS) * g


def _softplus(x):
    return jnp.maximum(x, 0.0) + jnp.log1p(jnp.exp(-jnp.abs(x)))


def _block_diag_dot(xb, w_ref):
    n = D_RNN // V7X_MXU_DIM
    return jnp.concatenate(
        [_dot(xb[:, i * V7X_MXU_DIM:(i + 1) * V7X_MXU_DIM], w_ref[i]) for i in range(n)], axis=1)


def _rglru_coeffs(xc, rwa_ref, rba_ref, rwx_ref, rbx_ref, lam_ref):
    xcb = xc.astype(BF16)
    r = jax.nn.sigmoid(_block_diag_dot(xcb, rwa_ref) + rba_ref[...])
    i = jax.nn.sigmoid(_block_diag_dot(xcb, rwx_ref) + rbx_ref[...])
    log_a = (-RG_C) * r * _softplus(-lam_ref[...])
    a = jnp.exp(log_a)
    b = jnp.sqrt(1.0 - a * a) * (i * xc)
    return a, b


def _head_norm_gate(o, gng, g_out):
    parts = []
    for hh in range(GLA_HEADS):
        sl = slice(hh * GLA_DV, (hh + 1) * GLA_DV)
        oh = o[:, sl]
        ms = jnp.mean(oh * oh, axis=-1, keepdims=True)
        parts.append(oh * lax.rsqrt(ms + EPS) * gng[:, sl])
    on = jnp.concatenate(parts, axis=1)
    return on * (g_out * jax.nn.sigmoid(g_out))


def _delayed_rows(prev, u, s):
    ext = jnp.concatenate([prev, u], axis=0)
    return pltpu.roll(ext, s, axis=0)[V7X_SUBLANES:]


def _causal_conv(u, prev, cw, cb, hist):
    y = cb + _delayed_rows(prev, u, hist) * cw[0:1]
    for j in range(1, hist):
        y = y + _delayed_rows(prev, u, hist - j) * cw[j:j + 1]
    return y + u * cw[hist:hist + 1]


def _scan_affine(a, b, h_init, tt):
    width = a.shape[1]
    n_tiles = tt // V7X_SUBLANES
    a = a.reshape(n_tiles, V7X_SUBLANES, width)
    b = b.reshape(n_tiles, V7X_SUBLANES, width)
    sub = lax.broadcasted_iota(jnp.int32, a.shape, 1)
    s = 1
    while s < V7X_SUBLANES:
        keep = sub >= s
        b = jnp.where(keep, a * pltpu.roll(b, s, axis=1) + b, b)
        a = jnp.where(keep, a * pltpu.roll(a, s, axis=1), a)
        s *= 2
    tiles = []
    carry = h_init
    for g in range(n_tiles):
        h = b[g] + a[g] * carry
        carry = h[V7X_SUBLANES - 1:V7X_SUBLANES, :]
        tiles.append(h)
    return jnp.concatenate(tiles, axis=0)


def _mixer_seq_kernel(x_ref, rgbuf0_ref, h0_ref, s0_ref, g1_ref, wa_ref, wglr_ref, wb_ref,
                      cw_ref, cb_ref, rwa_ref, rba_ref, rwx_ref, rbx_ref, lam_ref,
                      wgate_ref, bgate_ref, gng_ref, wbra_ref, wbrb_ref, wout_ref,
                      xo_ref, rgbuf_ref, h_ref, s_ref,
                      hist_ref, o_ref, *, tt, chunk):
    hist = RG_CONV - 1

    @pl.when(pl.program_id(1) == 0)
    def _():
        hist_ref[...] = jnp.zeros(hist_ref.shape, F32)
        hist_ref[V7X_SUBLANES - hist:V7X_SUBLANES, :] = rgbuf0_ref[...]
        h_ref[...] = h0_ref[...]
        s_ref[...] = s0_ref[...]

    x = x_ref[...]
    xn = _rmsnorm(x, g1_ref[...]).astype(BF16)

    rg_x = _dot(xn, wa_ref[:, 0:D_RNN])
    xc = _causal_conv(rg_x, hist_ref[...], cw_ref[...], cb_ref[...], hist)
    hist_ref[...] = rg_x[tt - V7X_SUBLANES:tt, :]
    rgbuf_ref[...] = hist_ref[V7X_SUBLANES - hist:V7X_SUBLANES, :]
    a, b = _rglru_coeffs(xc, rwa_ref, rba_ref, rwx_ref, rbx_ref, lam_ref)

    glr = _dot(xn, wglr_ref[...])
    qk = _dot(xn, wa_ref[:, Q_OFF:V_OFF])
    rg_y = _dot(xn, wa_ref[:, D_RNN:2 * D_RNN])
    v = _dot(xn, wa_ref[:, V_OFF:V_OFF + V_DIM]).astype(BF16)
    h = _scan_affine(a, b, h_ref[...], tt)
    h_ref[...] = h[tt - 1:tt, :]
    ya = _dot((h * jax.nn.gelu(rg_y)).astype(BF16), wbra_ref[...])

    zg = _dot(glr.astype(BF16), wgate_ref[...]) + bgate_ref[...]
    la = -_softplus(-zg) * (1.0 / GLA_TAU)
    blk = min(tt, V7X_MXU_DIM)
    blocks = [slice(r0, r0 + blk) for r0 in range(0, tt, blk)]
    shift = chunk.bit_length() - 1
    r_i = lax.broadcasted_iota(jnp.int32, (blk, blk), 0)
    c_i = lax.broadcasted_iota(jnp.int32, (blk, blk), 1)
    causal = ((r_i >> shift) == (c_i >> shift)) & (c_i <= r_i)
    cum = jnp.where(causal, 1.0, 0.0).astype(BF16)
    la_hi = la.astype(BF16)
    la_lo = (la - la_hi.astype(F32)).astype(BF16)
    bc = jnp.concatenate([_dot(cum, la_hi[rb]) + _dot(cum, la_lo[rb]) for rb in blocks],
                         axis=0)
    qd_all = (qk[:, 0:QK_DIM] * (GLA_DK ** -0.5) * jnp.exp(bc)).astype(BF16)
    kd_all = (qk[:, QK_DIM:2 * QK_DIM] * jnp.exp(-bc)).astype(BF16)
    zb = _dot(xn, wb_ref[...])
    for hh in range(GLA_HEADS):
        ks = slice(hh * GLA_DK, (hh + 1) * GLA_DK)
        vs = slice(hh * GLA_DV, (hh + 1) * GLA_DV)
        for rb in blocks:
            att = lax.dot_general(qd_all[rb, ks], kd_all[rb, ks], (((1,), (1,)), ((), ())),
                                  preferred_element_type=F32)
            o_intra = _dot(jnp.where(causal, att, 0.0).astype(BF16), v[rb, vs])
            for c0 in range(0, blk, chunk):
                rs = slice(rb.start + c0, rb.start + c0 + chunk)
                g = bc[rs, ks]
                gl = g[chunk - 1:chunk, :]
                ke = (qk[rs, QK_DIM + hh * GLA_DK:QK_DIM + (hh + 1) * GLA_DK]
                      * jnp.exp(gl - g)).astype(BF16)
                st = s_ref[hh]
                o_ref[rs, vs] = o_intra[c0:c0 + chunk] + _dot(qd_all[rs, ks], st.astype(BF16))
                ds = lax.dot_general(ke, v[rs, vs], (((0,), (0,)), ((), ())),
                                     preferred_element_type=F32)
                dec = jnp.broadcast_to(jnp.exp(gl), (GLA_DK, GLA_DK)).T
                s_ref[hh] = jnp.concatenate([dec] * (GLA_DV // GLA_DK), axis=1) * st + ds

    on = _head_norm_gate(o_ref[...], gng_ref[...], zb[:, 0:V_DIM])
    yb = _dot(on.astype(BF16), wbrb_ref[...])
    merged = (jax.nn.sigmoid(zb[:, V_DIM:V_DIM + D_MODEL]) * ya
              + jax.nn.sigmoid(zb[:, V_DIM + D_MODEL:V_DIM + 2 * D_MODEL]) * yb)
    xo_ref[...] = x + _dot(merged.astype(BF16), wout_ref[...])


def _ffn_seq_kernel(x_ref, buf0_ref, g2_ref, wup_ref, cw_ref, cb_ref, wdn_ref, gf_ref,
                    xo_ref, buf_ref, hist_ref, *, tt, cf, final):
    hist = FFN_CONV - 1

    @pl.when(pl.program_id(1) == 0)
    def _():
        hist_ref[...] = jnp.zeros(hist_ref.shape, F32)
        hist_ref[V7X_SUBLANES - hist:V7X_SUBLANES, :] = buf0_ref[...]

    x = x_ref[...]
    xn = _rmsnorm(x, g2_ref[...]).astype(BF16)
    def up_proj(j):
        return [(slice(off, off + cf), _dot(xn, wup_ref[:, off:off + cf]))
                for off in (j * cf, D_FF + j * cf)]

    acc = x
    n_chunks = D_FF // cf
    ups = up_proj(0)
    for j in range(n_chunks):
        nxt = up_proj(j + 1) if j + 1 < n_chunks else None
        halves = []
        for cs, u in ups:
            halves.append(_causal_conv(u, hist_ref[:, cs], cw_ref[:, cs], cb_ref[:, cs], hist))
            hist_ref[:, cs] = u[tt - V7X_SUBLANES:tt, :]
        hmid = (jax.nn.gelu(halves[0]) * halves[1]).astype(BF16)
        acc = acc + _dot(hmid, wdn_ref[j * cf:(j + 1) * cf, :])
        ups = nxt
    buf_ref[...] = hist_ref[V7X_SUBLANES - hist:V7X_SUBLANES, :]
    xo_ref[...] = _rmsnorm(acc, gf_ref[...]) if final else acc


def _layer_spec(tail, l, single=True):
    zeros = (0,) * len(tail)
    kw = dict(pipeline_mode=pl.Buffered(1)) if single else {}
    return pl.BlockSpec((None,) + tuple(tail), lambda *_: (l,) + zeros, **kw)


def _mixer_seq(x, rgbuf0, h0, s0, w, l, tt, chunk):
    bsz, t_len, _ = x.shape
    bi = 1 if rgbuf0.shape[0] == bsz else 0

    def state_spec(tail):
        zeros = (0,) * len(tail)
        return pl.BlockSpec((None,) + tuple(tail), lambda b, t: (b * bi,) + zeros)

    def out_state_spec(tail):
        zeros = (0,) * len(tail)
        return pl.BlockSpec((None,) + tuple(tail), lambda b, t: (b,) + zeros)

    x_spec = pl.BlockSpec((None, tt, D_MODEL), lambda b, t: (b, t, 0))
    hist = RG_CONV - 1
    names = ("norm1_g", "w_a", "w_glr", "w_b", "rg_conv_w", "rg_conv_b", "rg_wa", "rg_ba",
             "rg_wx", "rg_bx", "rg_lambda", "gla_w_gate", "gla_b_gate", "gla_norm_g",
             "w_branch_a", "w_branch_b", "w_out")
    return pl.pallas_call(
        functools.partial(_mixer_seq_kernel, tt=tt, chunk=chunk),
        grid=(bsz, t_len // tt),
        in_specs=[x_spec, state_spec((hist, D_RNN)), state_spec((1, D_RNN)),
                  state_spec((GLA_HEADS, GLA_DK, GLA_DV))]
                 + [_layer_spec(w[n].shape[1:], l) for n in names],
        out_specs=[x_spec, out_state_spec((hist, D_RNN)), out_state_spec((1, D_RNN)),
                   out_state_spec((GLA_HEADS, GLA_DK, GLA_DV))],
        out_shape=[jax.ShapeDtypeStruct(x.shape, F32),
                   jax.ShapeDtypeStruct((bsz, hist, D_RNN), F32),
                   jax.ShapeDtypeStruct((bsz, 1, D_RNN), F32),
                   jax.ShapeDtypeStruct((bsz, GLA_HEADS, GLA_DK, GLA_DV), F32)],
        scratch_shapes=[pltpu.VMEM((V7X_SUBLANES, D_RNN), F32),
                        pltpu.VMEM((tt, V_DIM), F32)],
        compiler_params=pltpu.CompilerParams(
            dimension_semantics=("arbitrary", "arbitrary"), vmem_limit_bytes=VMEM_LIMIT),
        name="mixer_seq",
    )(x, rgbuf0, h0, s0, *[w[n] for n in names])


def _ffn_seq(x, buf0, w, l, tt, final):
    bsz, t_len, _ = x.shape
    bi = 1 if buf0.shape[0] == bsz else 0
    hist = FFN_CONV - 1
    cf = FFN_COL_TILE
    x_spec = pl.BlockSpec((None, tt, D_MODEL), lambda b, t: (b, t, 0))
    names = ("norm2_g", "w_ffn_up", "ffn_conv_w", "ffn_conv_b", "w_ffn_down")
    return pl.pallas_call(
        functools.partial(_ffn_seq_kernel, tt=tt, cf=cf, final=final),
        grid=(bsz, t_len // tt),
        in_specs=[x_spec, pl.BlockSpec((None, hist, 2 * D_FF), lambda b, t: (b * bi, 0, 0))]
                 + [_layer_spec(w[n].shape[1:], l) for n in names]
                 + [pl.BlockSpec((1, D_MODEL), lambda b, t: (0, 0))],
        out_specs=[x_spec, pl.BlockSpec((None, hist, 2 * D_FF), lambda b, t: (b, 0, 0))],
        out_shape=[jax.ShapeDtypeStruct(x.shape, F32),
                   jax.ShapeDtypeStruct((bsz, hist, 2 * D_FF), F32)],
        scratch_shapes=[pltpu.VMEM((V7X_SUBLANES, 2 * D_FF), F32)],
        compiler_params=pltpu.CompilerParams(
            dimension_semantics=("arbitrary", "arbitrary"), vmem_limit_bytes=VMEM_LIMIT),
        name="ffn_seq",
    )(x, buf0, *[w[n] for n in names], w["final_norm_g"])


def _trunk_seq(x, states, w, tt, chunk):
    outs = []
    for l in range(DEPTH):
        rgbuf0, h0, s0, fbuf0 = states[l]
        x, rgbuf, h, s = _mixer_seq(x, rgbuf0, h0, s0, w, l, tt, chunk)
        x, fbuf = _ffn_seq(x, fbuf0, w, l, tt, final=(l == DEPTH - 1))
        outs.append((rgbuf, h, s, fbuf))
    return x, outs


def _sample_pre_kernel(x_ref, rb0_ref, rb1_ref, rb2_ref, h0_ref, g1_ref, wa_ref, wglr_ref, wb_ref,
                       cw_ref, cb_ref, rwa_ref, rba_ref, rwx_ref, rbx_ref, lam_ref,
                       wgate_ref, bgate_ref, wbra_ref,
                       rgx_ref, h_ref, ya_ref, at_ref, kt_ref, wt_ref, v_ref, okv_ref, zb_ref):
    hist = RG_CONV - 1
    rb_refs = (rb0_ref, rb1_ref, rb2_ref)
    xn = _rmsnorm(x_ref[...], g1_ref[...]).astype(BF16)
    za = _dot(xn, wa_ref[...])
    glr = _dot(xn, wglr_ref[...])
    zb_ref[...] = _dot(xn, wb_ref[...])

    rg_x = za[:, 0:D_RNN]
    rgx_ref[...] = rg_x
    cw = cw_ref[...]
    xc = cb_ref[...] + rb_refs[0][...] * cw[0:1]
    for j in range(1, hist):
        xc = xc + rb_refs[j][...] * cw[j:j + 1]
    xc = xc + rg_x * cw[hist:hist + 1]
    a, b = _rglru_coeffs(xc, rwa_ref, rba_ref, rwx_ref, rbx_ref, lam_ref)
    h = a * h0_ref[...] + b
    h_ref[...] = h
    ya_ref[...] = _dot((h * jax.nn.gelu(za[:, D_RNN:2 * D_RNN])).astype(BF16), wbra_ref[...])

    zg = _dot(glr.astype(BF16), wgate_ref[...]) + bgate_ref[...]
    alpha = jnp.exp(-_softplus(-zg) * (1.0 / GLA_TAU))
    q = za[:, Q_OFF:Q_OFF + QK_DIM] * (GLA_DK ** -0.5)
    k = za[:, K_OFF:K_OFF + QK_DIM]
    v = za[:, V_OFF:V_OFF + V_DIM]
    at_ref[...] = alpha.T
    kt_ref[...] = k.T
    wt_ref[...] = (q * alpha).T
    v_ref[...] = v
    qk = q * k
    okv_ref[...] = jnp.concatenate(
        [jnp.sum(qk[:, hh * GLA_DK:(hh + 1) * GLA_DK], axis=-1, keepdims=True)
         * v[:, hh * GLA_DV:(hh + 1) * GLA_DV] for hh in range(GLA_HEADS)], axis=1)


def _row_onehot(nb, row):
    rows = lax.broadcasted_iota(jnp.int32, (nb, GLA_DV), 0)
    return jnp.where(rows == row, 1.0, 0.0).astype(BF16)


def _sample_read_kernel(s0_ref, wt_ref, okv_ref, o_ref, *, nb):
    i = pl.program_id(0)
    wb = wt_ref[...].astype(BF16)
    for j in range(SAMPLE_ROWS):
        cw = _dot(wb, _row_onehot(nb, i * SAMPLE_ROWS + j))
        for hh in range(GLA_HEADS):
            ks = slice(hh * GLA_DK, (hh + 1) * GLA_DK)
            vs = slice(hh * GLA_DV, (hh + 1) * GLA_DV)
            o_ref[j:j + 1, vs] = (jnp.sum(cw[ks] * s0_ref[j, hh], axis=0, keepdims=True)
                                  + okv_ref[j:j + 1, vs])


def _sample_update_kernel(s0_ref, at_ref, kt_ref, v_ref, s_ref, *, nb):
    i = pl.program_id(1)
    al = at_ref[...]
    a_hi = al.astype(BF16)
    r1 = al - a_hi.astype(F32)
    a_mid = r1.astype(BF16)
    a_lo = (r1 - a_mid.astype(F32)).astype(BF16)
    kb = kt_ref[...].astype(BF16)
    for j in range(SAMPLE_ROWS):
        e = _row_onehot(nb, i * SAMPLE_ROWS + j)
        ca = _dot(a_hi, e) + _dot(a_mid, e) + _dot(a_lo, e)
        ck = _dot(kb, e)
        for hh in range(GLA_HEADS):
            ks = slice(hh * GLA_DK, (hh + 1) * GLA_DK)
            vs = slice(hh * GLA_DV, (hh + 1) * GLA_DV)
            s_ref[j, hh] = ca[ks] * s0_ref[j, hh] + ck[ks] * v_ref[j:j + 1, vs]


def _sample_post_kernel(x_ref, ya_ref, o_ref, zb_ref, gng_ref, wbrb_ref, wout_ref, g2_ref,
                        wupg_ref, wupv_ref, cwg_ref, cwv_ref, cbg_ref, cbv_ref,
                        bg0_ref, bv0_ref, bg1_ref, bv1_ref, wdn_ref, gf_ref,
                        xo_ref, ug_ref, uv_ref, xn_ref, *, final):
    j = pl.program_id(0)

    @pl.when(j == 0)
    def _():
        zb = zb_ref[...]
        on = _head_norm_gate(o_ref[...], gng_ref[...], zb[:, 0:V_DIM])
        yb = _dot(on.astype(BF16), wbrb_ref[...])
        merged = (jax.nn.sigmoid(zb[:, V_DIM:V_DIM + D_MODEL]) * ya_ref[...]
                  + jax.nn.sigmoid(zb[:, V_DIM + D_MODEL:V_DIM + 2 * D_MODEL]) * yb)
        x1 = x_ref[...] + _dot(merged.astype(BF16), wout_ref[...])
        xo_ref[...] = x1
        xn_ref[...] = _rmsnorm(x1, g2_ref[...]).astype(BF16)

    xn = xn_ref[...]
    ug = _dot(xn, wupg_ref[...])
    uv = _dot(xn, wupv_ref[...])
    ug_ref[...] = ug
    uv_ref[...] = uv
    cwg = cwg_ref[...]
    cwv = cwv_ref[...]
    gate = cbg_ref[...] + bg0_ref[...] * cwg[0:1] + bg1_ref[...] * cwg[1:2] + ug * cwg[2:3]
    val = cbv_ref[...] + bv0_ref[...] * cwv[0:1] + bv1_ref[...] * cwv[1:2] + uv * cwv[2:3]
    xo_ref[...] += _dot((jax.nn.gelu(gate) * val).astype(BF16), wdn_ref[...])

    if final:
        @pl.when(j == pl.num_programs(0) - 1)
        def _():
            xo_ref[...] = _rmsnorm(xo_ref[...], gf_ref[...])


def _full_spec(shape):
    zeros = (0,) * len(shape)
    return pl.BlockSpec(tuple(shape), lambda *_: zeros)


def _sample_layer(x, rg_rows, h_all, s_all, ffn_rows, w, l, final):
    nb = x.shape[0]
    params = pltpu.CompilerParams(dimension_semantics=("arbitrary",),
                                  vmem_limit_bytes=VMEM_LIMIT)

    pre_names = ("norm1_g", "w_a", "w_glr", "w_b", "rg_conv_w", "rg_conv_b", "rg_wa", "rg_ba",
                 "rg_wx", "rg_bx", "rg_lambda", "gla_w_gate", "gla_b_gate", "w_branch_a")
    pre_out = [(nb, D_RNN), (nb, D_RNN), (nb, D_MODEL), (QK_DIM, nb), (QK_DIM, nb), (QK_DIM, nb),
               (nb, V_DIM), (nb, V_DIM), (nb, W_B_COLS)]
    rg_x, h, ya, a_t, k_t, w_t, v, okv, zb = pl.pallas_call(
        _sample_pre_kernel,
        grid=(1,),
        in_specs=[_full_spec(x.shape)]
                 + [_layer_spec(r.shape[1:], l, single=False) for r in rg_rows]
                 + [_layer_spec(h_all.shape[1:], l, single=False)]
                 + [_layer_spec(w[n].shape[1:], l) for n in pre_names],
        out_specs=[_full_spec(s) for s in pre_out],
        out_shape=[jax.ShapeDtypeStruct(s, F32) for s in pre_out],
        compiler_params=params,
        name="sample_pre",
    )(x, *rg_rows, h_all, *[w[n] for n in pre_names])

    s_blk = (None, SAMPLE_ROWS, GLA_HEADS, GLA_DK, GLA_DV)
    row_spec = pl.BlockSpec((SAMPLE_ROWS, V_DIM), lambda i: (i, 0))
    o = pl.pallas_call(
        functools.partial(_sample_read_kernel, nb=nb),
        grid=(nb // SAMPLE_ROWS,),
        in_specs=[pl.BlockSpec(s_blk, lambda i: (l, i, 0, 0, 0)), _full_spec((QK_DIM, nb)),
                  row_spec],
        out_specs=row_spec,
        out_shape=jax.ShapeDtypeStruct((nb, V_DIM), F32),
        compiler_params=params,
        name="sample_read",
    )(s_all, w_t, okv)

    tf = SAMPLE_FF_TILE
    nf = D_FF // tf

    def lcol(rows, off):
        return pl.BlockSpec((None, rows, tf), lambda j: (l, 0, off + j))

    ucol = pl.BlockSpec((nb, tf), lambda j: (0, j))
    xo, ug, uv = pl.pallas_call(
        functools.partial(_sample_post_kernel, final=final),
        grid=(nf,),
        in_specs=[_full_spec(x.shape), _full_spec(ya.shape), _full_spec(o.shape),
                  _full_spec(zb.shape),
                  _layer_spec(w["gla_norm_g"].shape[1:], l, single=False),
                  _layer_spec(w["w_branch_b"].shape[1:], l, single=False),
                  _layer_spec(w["w_out"].shape[1:], l, single=False),
                  _layer_spec(w["norm2_g"].shape[1:], l, single=False),
                  lcol(D_MODEL, 0), lcol(D_MODEL, nf),
                  lcol(FFN_CONV, 0), lcol(FFN_CONV, nf),
                  lcol(1, 0), lcol(1, nf),
                  lcol(nb, 0), lcol(nb, nf), lcol(nb, 0), lcol(nb, nf),
                  pl.BlockSpec((None, tf, D_MODEL), lambda j: (l, j, 0)),
                  _full_spec((1, D_MODEL))],
        out_specs=[_full_spec(x.shape), ucol, ucol],
        out_shape=[jax.ShapeDtypeStruct(x.shape, F32),
                   jax.ShapeDtypeStruct((nb, D_FF), F32), jax.ShapeDtypeStruct((nb, D_FF), F32)],
        scratch_shapes=[pltpu.VMEM((nb, D_MODEL), BF16)],
        compiler_params=params,
        name="sample_post",
    )(x, ya, o, zb, w["gla_norm_g"], w["w_branch_b"], w["w_out"], w["norm2_g"],
      w["w_ffn_up"], w["w_ffn_up"], w["ffn_conv_w"], w["ffn_conv_w"],
      w["ffn_conv_b"], w["ffn_conv_b"], ffn_rows[0], ffn_rows[0], ffn_rows[1], ffn_rows[1],
      w["w_ffn_down"], w["final_norm_g"])
    return xo, rg_x, h, (a_t, k_t, v), (ug, uv)


def _sample_state_update(s_all, a_t, k_t, v):
    nb = v.shape[1]
    s_blk = (None, SAMPLE_ROWS, GLA_HEADS, GLA_DK, GLA_DV)
    s_spec = pl.BlockSpec(s_blk, lambda l, i: (l, i, 0, 0, 0))
    t_spec = pl.BlockSpec((None, QK_DIM, nb), lambda l, i: (l, 0, 0))
    return pl.pallas_call(
        functools.partial(_sample_update_kernel, nb=nb),
        grid=(DEPTH, nb // SAMPLE_ROWS),
        in_specs=[s_spec, t_spec, t_spec,
                  pl.BlockSpec((None, SAMPLE_ROWS, V_DIM), lambda l, i: (l, i, 0))],
        out_specs=s_spec,
        out_shape=jax.ShapeDtypeStruct(s_all.shape, F32),
        compiler_params=pltpu.CompilerParams(
            dimension_semantics=("arbitrary", "arbitrary"), vmem_limit_bytes=VMEM_LIMIT),
        name="sample_update",
    )(s_all, a_t, k_t, v)


def _prep_weights(norm1_g, w_in, rg_conv_w, rg_conv_b, rg_wa, rg_ba, rg_wx, rg_bx, rg_lambda,
                  gla_w_gate, gla_b_gate, gla_norm_g, w_branch_a, w_branch_b, w_out, norm2_g,
                  w_ffn_up, ffn_conv_w, ffn_conv_b, w_ffn_down, final_norm_g):
    def vec(a):
        return a.reshape(DEPTH, 1, -1).astype(F32)

    def block_diag(wb):
        per = V7X_MXU_DIM // RG_BW
        w5 = wb.reshape(DEPTH, RG_BLOCKS // per, per, RG_BW, RG_BW)
        bd = jnp.einsum("lgicd,ij->lgicjd", w5, jnp.eye(per, dtype=wb.dtype))
        return bd.reshape(DEPTH, RG_BLOCKS // per, V7X_MXU_DIM, V7X_MXU_DIM).astype(BF16)

    pad_r = V7X_LANES - GLA_RANK
    return dict(
        norm1_g=vec(norm1_g),
        w_a=w_in[:, :, 0:W_A_COLS].astype(BF16),
        w_glr=jnp.pad(w_in[:, :, W_GLR_OFF:W_B_OFF], ((0, 0), (0, 0), (0, pad_r))).astype(BF16),
        w_b=w_in[:, :, W_B_OFF:W_B_OFF + W_B_COLS].astype(BF16),
        rg_conv_w=rg_conv_w.astype(F32), rg_conv_b=vec(rg_conv_b),
        rg_wa=block_diag(rg_wa), rg_ba=vec(rg_ba), rg_wx=block_diag(rg_wx), rg_bx=vec(rg_bx),
        rg_lambda=vec(rg_lambda),
        gla_w_gate=jnp.pad(gla_w_gate, ((0, 0), (0, pad_r), (0, 0))).astype(BF16),
        gla_b_gate=vec(gla_b_gate), gla_norm_g=vec(gla_norm_g),
        w_branch_a=w_branch_a.astype(BF16), w_branch_b=w_branch_b.astype(BF16),
        w_out=w_out.astype(BF16), norm2_g=vec(norm2_g),
        w_ffn_up=w_ffn_up.astype(BF16), ffn_conv_w=ffn_conv_w.astype(F32),
        ffn_conv_b=vec(ffn_conv_b), w_ffn_down=w_ffn_down.astype(BF16),
        final_norm_g=final_norm_g.reshape(1, D_MODEL).astype(F32),
    )


def kernel(x_prompt, x_sample, state_rg_conv, state_rg_h, state_gla, state_ffn_conv, meta_tokens, norm1_g, w_in, rg_conv_w, rg_conv_b, rg_wa, rg_ba, rg_wx, rg_bx, rg_lambda, gla_w_gate, gla_b_gate, gla_norm_g, w_branch_a, w_branch_b, w_out, norm2_g, w_ffn_up, ffn_conv_w, ffn_conv_b, w_ffn_down, final_norm_g):
    w = _prep_weights(norm1_g, w_in, rg_conv_w, rg_conv_b, rg_wa, rg_ba, rg_wx, rg_bx, rg_lambda,
                      gla_w_gate, gla_b_gate, gla_norm_g, w_branch_a, w_branch_b, w_out, norm2_g,
                      w_ffn_up, ffn_conv_w, ffn_conv_b, w_ffn_down, final_norm_g)
    bsz = x_prompt.shape[0]

    zero_states = [(jnp.zeros((1, RG_CONV - 1, D_RNN), F32), jnp.zeros((1, 1, D_RNN), F32),
                    jnp.zeros((1, GLA_HEADS, GLA_DK, GLA_DV), F32),
                    jnp.zeros((1, FFN_CONV - 1, 2 * D_FF), F32))] * DEPTH
    _, meta_states = _trunk_seq(meta_tokens.astype(F32)[None], zero_states, w,
                                tt=N_META, chunk=N_META)

    y_prompt, p_states = _trunk_seq(x_prompt, meta_states, w, tt=SEQ_TILE, chunk=GLA_CHUNK)
    rg_conv_prompt = jnp.stack([s[0] for s in p_states])
    rg_h_prompt = jnp.stack([s[1].reshape(bsz, D_RNN) for s in p_states])
    gla_prompt = jnp.stack([s[2] for s in p_states])
    ffn_conv_prompt = jnp.stack([s[3] for s in p_states])

    nb = x_sample.shape[0]
    xs = x_sample.reshape(nb, D_MODEL)
    rg_rows = [state_rg_conv[:, :, j] for j in range(RG_CONV - 1)]
    ffn_rows = [state_ffn_conv[:, :, j] for j in range(FFN_CONV - 1)]
    rg_xs, hs, akv, us = [], [], [], []
    for l in range(DEPTH):
        xs, rg_x, h, akv_l, u_l = _sample_layer(xs, rg_rows, state_rg_h, state_gla, ffn_rows,
                                                w, l, final=(l == DEPTH - 1))
        rg_xs.append(rg_x)
        hs.append(h)
        akv.append(akv_l)
        us.append(jnp.concatenate(u_l, axis=1))
    gla_sample = _sample_state_update(state_gla, *[jnp.stack(t) for t in zip(*akv)])
    rg_conv_sample = jnp.stack(rg_rows[1:] + [jnp.stack(rg_xs)], axis=2)
    ffn_conv_sample = jnp.stack(ffn_rows[1:] + [jnp.stack(us)], axis=2)
    y_sample = xs.reshape(x_sample.shape)
    return (y_prompt, y_sample, rg_conv_prompt, rg_h_prompt, gla_prompt, ffn_conv_prompt,
            rg_conv_sample, jnp.stack(hs), gla_sample, ffn_conv_sample)
```

```python
import functools

import jax
import jax.numpy as jnp
from jax import lax
from jax.experimental import pallas as pl
from jax.experimental.pallas import tpu as pltpu

F32 = jnp.float32
BF16 = jnp.bfloat16

D_MODEL = 1024
DEPTH = 4
D_RNN = D_MODEL
RG_BLOCKS = 16
RG_BW = D_RNN // RG_BLOCKS
RG_CONV = 4
RG_C = 8.0
GLA_HEADS = 4
GLA_DK = 128
GLA_DV = 256
QK_DIM = GLA_HEADS * GLA_DK
V_DIM = GLA_HEADS * GLA_DV
GLA_RANK = 16
GLA_TAU = 16.0
GLA_CHUNK = 64
D_FF = 3 * D_MODEL
FFN_CONV = 3
N_META = 16
EPS = 1e-6

V7X_MXU_DIM = 256
V7X_LANES = 128
V7X_SUBLANES = 8

W_A_COLS = 2 * D_RNN + 2 * QK_DIM + V_DIM
W_GLR_OFF = W_A_COLS
W_B_OFF = W_A_COLS + GLA_RANK
W_B_COLS = V_DIM + 2 * D_MODEL
Q_OFF, K_OFF, V_OFF = 2 * D_RNN, 2 * D_RNN + QK_DIM, 2 * D_RNN + 2 * QK_DIM

SEQ_TILE = 512
FFN_SEQ_TILE = 1024
FFN_COL_TILE = 1024
SAMPLE_ROWS = 8
SAMPLE_FF_TILE = 512
VMEM_LIMIT = 56 * 1024 * 1024
PIPE_VMEM_LIMIT = 60 * 1024 * 1024


def _dot(a, b):
    return jnp.dot(a, b, preferred_element_type=F32)


def _rmsnorm(x, g):
    ms = jnp.mean(x * x, axis=-1, keepdims=True)
    return x * lax.rsqrt(ms + EPS) * g


def _softplus(x):
    return jnp.maximum(x, 0.0) + jnp.log1p(jnp.exp(-jnp.abs(x)))


def _block_diag_dot(xb, w_ref):
    n = D_RNN // V7X_MXU_DIM
    return jnp.concatenate(
        [_dot(xb[:, i * V7X_MXU_DIM:(i + 1) * V7X_MXU_DIM], w_ref[i]) for i in range(n)], axis=1)


def _rglru_coeffs(xc, rwa_ref, rba_ref, rwx_ref, rbx_ref, lam_ref):
    xcb = xc.astype(BF16)
    r = jax.nn.sigmoid(_block_diag_dot(xcb, rwa_ref) + rba_ref[...])
    i = jax.nn.sigmoid(_block_diag_dot(xcb, rwx_ref) + rbx_ref[...])
    log_a = (-RG_C) * r * _softplus(-lam_ref[...])
    a = jnp.exp(log_a)
    y = 1.0 - a * a
    b = jnp.where(y > 0.0, y * lax.rsqrt(y), 0.0) * (i * xc)
    return a, b


def _head_norm_gate(o, gng, g_out):
    parts = []
    for hh in range(GLA_HEADS):
        sl = slice(hh * GLA_DV, (hh + 1) * GLA_DV)
        oh = o[:, sl]
        ms = jnp.mean(oh * oh, axis=-1, keepdims=True)
        parts.append(oh * lax.rsqrt(ms + EPS) * gng[:, sl])
    on = jnp.concatenate(parts, axis=1)
    return on * (g_out * jax.nn.sigmoid(g_out))


def _delayed_rows(prev, u, s):
    ext = jnp.concatenate([prev, u], axis=0)
    return pltpu.roll(ext, s, axis=0)[V7X_SUBLANES:]


def _causal_conv(u, prev, cw, cb, hist):
    y = cb + _delayed_rows(prev, u, hist) * cw[0:1]
    for j in range(1, hist):
        y = y + _delayed_rows(prev, u, hist - j) * cw[j:j + 1]
    return y + u * cw[hist:hist + 1]


def _scan_affine(a, b, h_init, tt):
    width = a.shape[1]
    n_tiles = tt // V7X_SUBLANES
    a = a.reshape(n_tiles, V7X_SUBLANES, width)
    b = b.reshape(n_tiles, V7X_SUBLANES, width)
    sub = lax.broadcasted_iota(jnp.int32, a.shape, 1)
    s = 1
    while s < V7X_SUBLANES:
        keep = sub >= s
        b = jnp.where(keep, a * pltpu.roll(b, s, axis=1) + b, b)
        a = jnp.where(keep, a * pltpu.roll(a, s, axis=1), a)
        s *= 2
    tiles = []
    carry = h_init
    for g in range(n_tiles):
        h = b[g] + a[g] * carry
        carry = h[V7X_SUBLANES - 1:V7X_SUBLANES, :]
        tiles.append(h)
    return jnp.concatenate(tiles, axis=0)


def _gla_decay(glr, wgate_ref, bgate_ref, tt, chunk):
    zg = _dot(glr.astype(BF16), wgate_ref[...]) + bgate_ref[...]
    la = -_softplus(-zg) * (1.0 / GLA_TAU)
    blk = min(tt, V7X_MXU_DIM)
    blocks = [slice(r0, r0 + blk) for r0 in range(0, tt, blk)]
    shift = chunk.bit_length() - 1
    r_i = lax.broadcasted_iota(jnp.int32, (blk, blk), 0)
    c_i = lax.broadcasted_iota(jnp.int32, (blk, blk), 1)
    causal = ((r_i >> shift) == (c_i >> shift)) & (c_i <= r_i)
    cum = jnp.where(causal, 1.0, 0.0).astype(BF16)
    la_hi = la.astype(BF16)
    la_lo = (la - la_hi.astype(F32)).astype(BF16)
    bc = jnp.concatenate([_dot(cum, la_hi[rb]) + _dot(cum, la_lo[rb]) for rb in blocks], axis=0)
    return bc, causal, blocks


def _gla_heads(qk, v, bc, causal, blocks, s_ref, o_ref, chunk):
    blk = blocks[0].stop
    qd_all = (qk[:, 0:QK_DIM] * (GLA_DK ** -0.5) * jnp.exp(bc)).astype(BF16)
    kd_all = (qk[:, QK_DIM:2 * QK_DIM] * jnp.exp(-bc)).astype(BF16)
    for hh in range(GLA_HEADS):
        ks = slice(hh * GLA_DK, (hh + 1) * GLA_DK)
        vs = slice(hh * GLA_DV, (hh + 1) * GLA_DV)
        for rb in blocks:
            att = lax.dot_general(qd_all[rb, ks], kd_all[rb, ks], (((1,), (1,)), ((), ())),
                                  preferred_element_type=F32)
            o_intra = _dot(jnp.where(causal, att, 0.0).astype(BF16), v[rb, vs])
            for c0 in range(0, blk, chunk):
                rs = slice(rb.start + c0, rb.start + c0 + chunk)
                g = bc[rs, ks]
                gl = g[chunk - 1:chunk, :]
                ke = (qk[rs, QK_DIM + hh * GLA_DK:QK_DIM + (hh + 1) * GLA_DK]
                      * jnp.exp(gl - g)).astype(BF16)
                st = s_ref[hh]
                o_ref[rs, vs] = o_intra[c0:c0 + chunk] + _dot(qd_all[rs, ks], st.astype(BF16))
                ds = lax.dot_general(ke, v[rs, vs], (((0,), (0,)), ((), ())),
                                     preferred_element_type=F32)
                dec = jnp.broadcast_to(jnp.exp(gl), (GLA_DK, GLA_DK)).T
                s_ref[hh] = jnp.concatenate([dec] * (GLA_DV // GLA_DK), axis=1) * st + ds


def _mixer_tail(x, zb, ya, o, gng_ref, wbrb_ref, wout_ref):
    on = _head_norm_gate(o, gng_ref[...], zb[:, 0:V_DIM])
    yb = _dot(on.astype(BF16), wbrb_ref[...])
    merged = (jax.nn.sigmoid(zb[:, V_DIM:V_DIM + D_MODEL]) * ya
              + jax.nn.sigmoid(zb[:, V_DIM + D_MODEL:V_DIM + 2 * D_MODEL]) * yb)
    return x + _dot(merged.astype(BF16), wout_ref[...])


def _mixer_seq_kernel(x_ref, rgbuf0_ref, h0_ref, s0_ref, g1_ref, wa_ref, wglr_ref, wb_ref,
                      cw_ref, cb_ref, rwa_ref, rba_ref, rwx_ref, rbx_ref, lam_ref,
                      wgate_ref, bgate_ref, gng_ref, wbra_ref, wbrb_ref, wout_ref,
                      xo_ref, rgbuf_ref, h_ref, s_ref,
                      hist_ref, o_ref, *, tt, chunk):
    hist = RG_CONV - 1

    @pl.when(pl.program_id(1) == 0)
    def _():
        hist_ref[...] = jnp.zeros(hist_ref.shape, F32)
        hist_ref[V7X_SUBLANES - hist:V7X_SUBLANES, :] = rgbuf0_ref[...]
        h_ref[...] = h0_ref[...]
        s_ref[...] = s0_ref[...]

    x = x_ref[...]
    xn = _rmsnorm(x, g1_ref[...]).astype(BF16)

    rg_x = _dot(xn, wa_ref[:, 0:D_RNN])
    glr = _dot(xn, wglr_ref[...])
    xc = _causal_conv(rg_x, hist_ref[...], cw_ref[...], cb_ref[...], hist)
    hist_ref[...] = rg_x[tt - V7X_SUBLANES:tt, :]
    rgbuf_ref[...] = hist_ref[V7X_SUBLANES - hist:V7X_SUBLANES, :]
    qk = _dot(xn, wa_ref[:, Q_OFF:V_OFF])
    a, b = _rglru_coeffs(xc, rwa_ref, rba_ref, rwx_ref, rbx_ref, lam_ref)
    rg_y = _dot(xn, wa_ref[:, D_RNN:2 * D_RNN])
    v = _dot(xn, wa_ref[:, V_OFF:V_OFF + V_DIM]).astype(BF16)
    h = _scan_affine(a, b, h_ref[...], tt)
    h_ref[...] = h[tt - 1:tt, :]
    zb = _dot(xn, wb_ref[...])
    ya = _dot((h * jax.nn.gelu(rg_y)).astype(BF16), wbra_ref[...])

    bc, causal, blocks = _gla_decay(glr, wgate_ref, bgate_ref, tt, chunk)
    _gla_heads(qk, v, bc, causal, blocks, s_ref, o_ref, chunk)
    xo_ref[...] = _mixer_tail(x, zb, ya, o_ref[...], gng_ref, wbrb_ref, wout_ref)


def _mixer_pipe_kernel(x_ref, xnext_ref, rgbuf0_ref, h0_ref, s0_ref, g1_ref, wa_ref, wglr_ref,
                       wb_ref, cw_ref, cb_ref, rwa_ref, rba_ref, rwx_ref, rbx_ref, lam_ref,
                       wgate_ref, bgate_ref, gng_ref, wbra_ref, wbrb_ref, wout_ref,
                       xo_ref, rgbuf_ref, h_ref, s_ref,
                       hist_ref, hcar_ref, hist0_ref, xn_ref, hg_ref, o_ref, *, tt, chunk, nt):
    hist = RG_CONV - 1
    i = pl.program_id(0)
    n = pl.num_programs(0)
    slot = lax.rem(i, 2)

    def rg_conv_in(x_new, first):
        xn_new = _rmsnorm(x_new, g1_ref[...]).astype(BF16)
        rg_x = _dot(xn_new, wa_ref[:, 0:D_RNN])
        prev = jnp.where(first, hist0_ref[...], hist_ref[...])
        return xn_new, rg_x, _causal_conv(rg_x, prev, cw_ref[...], cb_ref[...], hist)

    def rg_scan(a, b, first):
        return _scan_affine(a, b, jnp.where(first, h0_ref[...], hcar_ref[...]), tt)

    def rg_finish(xn_new, rg_x, h, commit, out_slot):
        rg_y = _dot(xn_new, wa_ref[:, D_RNN:2 * D_RNN])
        xn_ref[out_slot] = xn_new
        hg_ref[out_slot] = (h * jax.nn.gelu(rg_y)).astype(BF16)
        hist_ref[...] = jnp.where(commit, rg_x[tt - V7X_SUBLANES:tt, :], hist_ref[...])
        hcar_ref[...] = jnp.where(commit, h[tt - 1:tt, :], hcar_ref[...])

    @pl.when(i == 0)
    def _():
        hist0_ref[...] = jnp.zeros(hist0_ref.shape, F32)
        hist0_ref[V7X_SUBLANES - hist:V7X_SUBLANES, :] = rgbuf0_ref[...]
        hist_ref[...] = hist0_ref[...]
        hcar_ref[...] = h0_ref[...]
        xn0, rg_x0, xc0 = rg_conv_in(x_ref[...], True)
        a0, b0 = _rglru_coeffs(xc0, rwa_ref, rba_ref, rwx_ref, rbx_ref, lam_ref)
        rg_finish(xn0, rg_x0, rg_scan(a0, b0, True), True, 0)

    @pl.when(lax.rem(i, nt) == 0)
    def _():
        s_ref[...] = s0_ref[...]

    nxt = i + 1
    first = lax.rem(nxt, nt) == 0
    commit = nxt < n
    x = x_ref[...]
    xn = xn_ref[slot]
    hg = hg_ref[slot]

    qk = _dot(xn, wa_ref[:, Q_OFF:V_OFF])
    glr = _dot(xn, wglr_ref[...])
    xn_new, rg_x, xc = rg_conv_in(xnext_ref[...], first)
    v = _dot(xn, wa_ref[:, V_OFF:V_OFF + V_DIM]).astype(BF16)
    a, b = _rglru_coeffs(xc, rwa_ref, rba_ref, rwx_ref, rbx_ref, lam_ref)
    zb = _dot(xn, wb_ref[...])
    ya = _dot(hg, wbra_ref[...])
    bc, causal, blocks = _gla_decay(glr, wgate_ref, bgate_ref, tt, chunk)
    h = rg_scan(a, b, first)
    _gla_heads(qk, v, bc, causal, blocks, s_ref, o_ref, chunk)
    rg_finish(xn_new, rg_x, h, commit, 1 - slot)
    rgbuf_ref[...] = hist_ref[V7X_SUBLANES - hist:V7X_SUBLANES, :]
    h_ref[...] = hcar_ref[...]
    xo_ref[...] = _mixer_tail(x, zb, ya, o_ref[...], gng_ref, wbrb_ref, wout_ref)


def _ffn_seq_kernel(x_ref, buf0_ref, g2_ref, wup_ref, cw_ref, cb_ref, wdn_ref, gf_ref,
                    xo_ref, buf_ref, hist_ref, *, tt, cf, final):
    hist = FFN_CONV - 1

    @pl.when(pl.program_id(1) == 0)
    def _():
        hist_ref[...] = jnp.zeros(hist_ref.shape, F32)
        hist_ref[V7X_SUBLANES - hist:V7X_SUBLANES, :] = buf0_ref[...]

    x = x_ref[...]
    xn = _rmsnorm(x, g2_ref[...]).astype(BF16)
    def up_proj(j):
        return [(slice(off, off + cf), _dot(xn, wup_ref[:, off:off + cf]))
                for off in (j * cf, D_FF + j * cf)]

    acc = x
    n_chunks = D_FF // cf
    ups = up_proj(0)
    for j in range(n_chunks):
        nxt = up_proj(j + 1) if j + 1 < n_chunks else None
        halves = []
        for cs, u in ups:
            halves.append(_causal_conv(u, hist_ref[:, cs], cw_ref[:, cs], cb_ref[:, cs], hist))
            hist_ref[:, cs] = u[tt - V7X_SUBLANES:tt, :]
        hmid = (jax.nn.gelu(halves[0]) * halves[1]).astype(BF16)
        acc = acc + _dot(hmid, wdn_ref[j * cf:(j + 1) * cf, :])
        ups = nxt
    buf_ref[...] = hist_ref[V7X_SUBLANES - hist:V7X_SUBLANES, :]
    xo_ref[...] = _rmsnorm(acc, gf_ref[...]) if final else acc


def _layer_spec(tail, l, single=True):
    zeros = (0,) * len(tail)
    kw = dict(pipeline_mode=pl.Buffered(1)) if single else {}
    return pl.BlockSpec((None,) + tuple(tail), lambda *_: (l,) + zeros, **kw)


def _mixer_seq(x, rgbuf0, h0, s0, w, l, tt, chunk):
    bsz, t_len, _ = x.shape
    bi = 1 if rgbuf0.shape[0] == bsz else 0

    def state_spec(tail):
        zeros = (0,) * len(tail)
        return pl.BlockSpec((None,) + tuple(tail), lambda b, t: (b * bi,) + zeros)

    def out_state_spec(tail):
        zeros = (0,) * len(tail)
        return pl.BlockSpec((None,) + tuple(tail), lambda b, t: (b,) + zeros)

    x_spec = pl.BlockSpec((None, tt, D_MODEL), lambda b, t: (b, t, 0))
    hist = RG_CONV - 1
    names = ("norm1_g", "w_a", "w_glr", "w_b", "rg_conv_w", "rg_conv_b", "rg_wa", "rg_ba",
             "rg_wx", "rg_bx", "rg_lambda", "gla_w_gate", "gla_b_gate", "gla_norm_g",
             "w_branch_a", "w_branch_b", "w_out")
    return pl.pallas_call(
        functools.partial(_mixer_seq_kernel, tt=tt, chunk=chunk),
        grid=(bsz, t_len // tt),
        in_specs=[x_spec, state_spec((hist, D_RNN)), state_spec((1, D_RNN)),
                  state_spec((GLA_HEADS, GLA_DK, GLA_DV))]
                 + [_layer_spec(w[n].shape[1:], l) for n in names],
        out_specs=[x_spec, out_state_spec((hist, D_RNN)), out_state_spec((1, D_RNN)),
                   out_state_spec((GLA_HEADS, GLA_DK, GLA_DV))],
        out_shape=[jax.ShapeDtypeStruct(x.shape, F32),
                   jax.ShapeDtypeStruct((bsz, hist, D_RNN), F32),
                   jax.ShapeDtypeStruct((bsz, 1, D_RNN), F32),
                   jax.ShapeDtypeStruct((bsz, GLA_HEADS, GLA_DK, GLA_DV), F32)],
        scratch_shapes=[pltpu.VMEM((V7X_SUBLANES, D_RNN), F32),
                        pltpu.VMEM((tt, V_DIM), F32)],
        compiler_params=pltpu.CompilerParams(
            dimension_semantics=("arbitrary", "arbitrary"), vmem_limit_bytes=VMEM_LIMIT),
        name="mixer_seq",
    )(x, rgbuf0, h0, s0, *[w[n] for n in names])


def _mixer_pipe(x, rgbuf0, h0, s0, w, l, tt, chunk):
    bsz, t_len, _ = x.shape
    assert rgbuf0.shape[0] == 1 and h0.shape[0] == 1 and s0.shape[0] == 1
    nt = t_len // tt
    n = bsz * nt
    hist = RG_CONV - 1

    def shared(tail):
        zeros = (0,) * len(tail)
        return pl.BlockSpec((None,) + tuple(tail), lambda i: (0,) + zeros)

    def next_row(tail):
        zeros = (0,) * len(tail)
        return pl.BlockSpec((None,) + tuple(tail),
                            lambda i: (jnp.minimum(i + 1, n - 1) // nt,) + zeros)

    tile = (None, tt, D_MODEL)
    names = ("norm1_g", "w_a", "w_glr", "w_b", "rg_conv_w", "rg_conv_b", "rg_wa", "rg_ba",
             "rg_wx", "rg_bx", "rg_lambda", "gla_w_gate", "gla_b_gate", "gla_norm_g",
             "w_branch_a", "w_branch_b", "w_out")
    xf = x.reshape(n, tt, D_MODEL)
    xo, rgbuf, h, s = pl.pallas_call(
        functools.partial(_mixer_pipe_kernel, tt=tt, chunk=chunk, nt=nt),
        grid=(n,),
        in_specs=[pl.BlockSpec(tile, lambda i: (i, 0, 0)),
                  pl.BlockSpec(tile, lambda i: (jnp.minimum(i + 1, n - 1), 0, 0)),
                  shared((hist, D_RNN)), shared((1, D_RNN)),
                  shared((GLA_HEADS, GLA_DK, GLA_DV))]
                 + [_layer_spec(w[k].shape[1:], l) for k in names],
        out_specs=[pl.BlockSpec(tile, lambda i: (i, 0, 0)),
                   next_row((hist, D_RNN)), next_row((1, D_RNN)),
                   pl.BlockSpec((None, GLA_HEADS, GLA_DK, GLA_DV), lambda i: (i // nt, 0, 0, 0))],
        out_shape=[jax.ShapeDtypeStruct(xf.shape, F32),
                   jax.ShapeDtypeStruct((bsz, hist, D_RNN), F32),
                   jax.ShapeDtypeStruct((bsz, 1, D_RNN), F32),
                   jax.ShapeDtypeStruct((bsz, GLA_HEADS, GLA_DK, GLA_DV), F32)],
        scratch_shapes=[pltpu.VMEM((V7X_SUBLANES, D_RNN), F32),
                        pltpu.VMEM((1, D_RNN), F32),
                        pltpu.VMEM((V7X_SUBLANES, D_RNN), F32),
                        pltpu.VMEM((2, tt, D_MODEL), BF16),
                        pltpu.VMEM((2, tt, D_RNN), BF16),
                        pltpu.VMEM((tt, V_DIM), F32)],
        compiler_params=pltpu.CompilerParams(
            dimension_semantics=("arbitrary",), vmem_limit_bytes=PIPE_VMEM_LIMIT),
        name="mixer_pipe",
    )(xf, xf, rgbuf0, h0, s0, *[w[k] for k in names])
    return xo.reshape(x.shape), rgbuf, h, s


def _ffn_seq(x, buf0, w, l, tt, final):
    bsz, t_len, _ = x.shape
    bi = 1 if buf0.shape[0] == bsz else 0
    hist = FFN_CONV - 1
    cf = FFN_COL_TILE
    x_spec = pl.BlockSpec((None, tt, D_MODEL), lambda b, t: (b, t, 0))
    names = ("norm2_g", "w_ffn_up", "ffn_conv_w", "ffn_conv_b", "w_ffn_down")
    return pl.pallas_call(
        functools.partial(_ffn_seq_kernel, tt=tt, cf=cf, final=final),
        grid=(bsz, t_len // tt),
        in_specs=[x_spec, pl.BlockSpec((None, hist, 2 * D_FF), lambda b, t: (b * bi, 0, 0))]
                 + [_layer_spec(w[n].shape[1:], l) for n in names]
                 + [pl.BlockSpec((1, D_MODEL), lambda b, t: (0, 0))],
        out_specs=[x_spec, pl.BlockSpec((None, hist, 2 * D_FF), lambda b, t: (b, 0, 0))],
        out_shape=[jax.ShapeDtypeStruct(x.shape, F32),
                   jax.ShapeDtypeStruct((bsz, hist, 2 * D_FF), F32)],
        scratch_shapes=[pltpu.VMEM((V7X_SUBLANES, 2 * D_FF), F32)],
        compiler_params=pltpu.CompilerParams(
            dimension_semantics=("arbitrary", "arbitrary"), vmem_limit_bytes=VMEM_LIMIT),
        name="ffn_seq",
    )(x, buf0, *[w[n] for n in names], w["final_norm_g"])


def _trunk_seq(x, states, w, tt, ffn_tt, chunk):
    outs = []
    for l in range(DEPTH):
        rgbuf0, h0, s0, fbuf0 = states[l]
        x, rgbuf, h, s = _mixer_seq(x, rgbuf0, h0, s0, w, l, tt, chunk)
        x, fbuf = _ffn_seq(x, fbuf0, w, l, ffn_tt, final=(l == DEPTH - 1))
        outs.append((rgbuf, h, s, fbuf))
    return x, outs


def _sample_pre_kernel(x_ref, rb0_ref, rb1_ref, rb2_ref, h0_ref, g1_ref, wa_ref, wglr_ref, wb_ref,
                       cw_ref, cb_ref, rwa_ref, rba_ref, rwx_ref, rbx_ref, lam_ref,
                       wgate_ref, bgate_ref, wbra_ref,
                       rgx_ref, h_ref, ya_ref, at_ref, kt_ref, wt_ref, v_ref, okv_ref, zb_ref):
    hist = RG_CONV - 1
    rb_refs = (rb0_ref, rb1_ref, rb2_ref)
    xn = _rmsnorm(x_ref[...], g1_ref[...]).astype(BF16)
    za = _dot(xn, wa_ref[...])
    glr = _dot(xn, wglr_ref[...])
    zb_ref[...] = _dot(xn, wb_ref[...])

    rg_x = za[:, 0:D_RNN]
    rgx_ref[...] = rg_x
    cw = cw_ref[...]
    xc = cb_ref[...] + rb_refs[0][...] * cw[0:1]
    for j in range(1, hist):
        xc = xc + rb_refs[j][...] * cw[j:j + 1]
    xc = xc + rg_x * cw[hist:hist + 1]
    a, b = _rglru_coeffs(xc, rwa_ref, rba_ref, rwx_ref, rbx_ref, lam_ref)
    h = a * h0_ref[...] + b
    h_ref[...] = h
    ya_ref[...] = _dot((h * jax.nn.gelu(za[:, D_RNN:2 * D_RNN])).astype(BF16), wbra_ref[...])

    zg = _dot(glr.astype(BF16), wgate_ref[...]) + bgate_ref[...]
    alpha = jnp.exp(-_softplus(-zg) * (1.0 / GLA_TAU))
    q = za[:, Q_OFF:Q_OFF + QK_DIM] * (GLA_DK ** -0.5)
    k = za[:, K_OFF:K_OFF + QK_DIM]
    v = za[:, V_OFF:V_OFF + V_DIM]
    at_ref[...] = alpha.T
    kt_ref[...] = k.T
    wt_ref[...] = (q * alpha).T
    v_ref[...] = v
    qk = q * k
    okv_ref[...] = jnp.concatenate(
        [jnp.sum(qk[:, hh * GLA_DK:(hh + 1) * GLA_DK], axis=-1, keepdims=True)
         * v[:, hh * GLA_DV:(hh + 1) * GLA_DV] for hh in range(GLA_HEADS)], axis=1)


def _row_onehot(nb, row):
    rows = lax.broadcasted_iota(jnp.int32, (nb, GLA_DV), 0)
    return jnp.where(rows == row, 1.0, 0.0).astype(BF16)


def _sample_read_kernel(s0_ref, wt_ref, okv_ref, o_ref, *, nb):
    i = pl.program_id(0)
    wb = wt_ref[...].astype(BF16)
    for j in range(SAMPLE_ROWS):
        cw = _dot(wb, _row_onehot(nb, i * SAMPLE_ROWS + j))
        for hh in range(GLA_HEADS):
            ks = slice(hh * GLA_DK, (hh + 1) * GLA_DK)
            vs = slice(hh * GLA_DV, (hh + 1) * GLA_DV)
            o_ref[j:j + 1, vs] = (jnp.sum(cw[ks] * s0_ref[j, hh], axis=0, keepdims=True)
                                  + okv_ref[j:j + 1, vs])


def _sample_update_kernel(s0_ref, at_ref, kt_ref, v_ref, s_ref, *, nb):
    i = pl.program_id(1)
    al = at_ref[...]
    a_hi = al.astype(BF16)
    r1 = al - a_hi.astype(F32)
    a_mid = r1.astype(BF16)
    a_lo = (r1 - a_mid.astype(F32)).astype(BF16)
    kb = kt_ref[...].astype(BF16)
    for j in range(SAMPLE_ROWS):
        e = _row_onehot(nb, i * SAMPLE_ROWS + j)
        ca = _dot(a_hi, e) + _dot(a_mid, e) + _dot(a_lo, e)
        ck = _dot(kb, e)
        for hh in range(GLA_HEADS):
            ks = slice(hh * GLA_DK, (hh + 1) * GLA_DK)
            vs = slice(hh * GLA_DV, (hh + 1) * GLA_DV)
            s_ref[j, hh] = ca[ks] * s0_ref[j, hh] + ck[ks] * v_ref[j:j + 1, vs]


def _sample_post_kernel(x_ref, ya_ref, o_ref, zb_ref, gng_ref, wbrb_ref, wout_ref, g2_ref,
                        wupg_ref, wupv_ref, cwg_ref, cwv_ref, cbg_ref, cbv_ref,
                        bg0_ref, bv0_ref, bg1_ref, bv1_ref, wdn_ref, gf_ref,
                        xo_ref, ug_ref, uv_ref, xn_ref, *, final):
    j = pl.program_id(0)

    @pl.when(j == 0)
    def _():
        zb = zb_ref[...]
        on = _head_norm_gate(o_ref[...], gng_ref[...], zb[:, 0:V_DIM])
        yb = _dot(on.astype(BF16), wbrb_ref[...])
        merged = (jax.nn.sigmoid(zb[:, V_DIM:V_DIM + D_MODEL]) * ya_ref[...]
                  + jax.nn.sigmoid(zb[:, V_DIM + D_MODEL:V_DIM + 2 * D_MODEL]) * yb)
        x1 = x_ref[...] + _dot(merged.astype(BF16), wout_ref[...])
        xo_ref[...] = x1
        xn_ref[...] = _rmsnorm(x1, g2_ref[...]).astype(BF16)

    xn = xn_ref[...]
    ug = _dot(xn, wupg_ref[...])
    uv = _dot(xn, wupv_ref[...])
    ug_ref[...] = ug
    uv_ref[...] = uv
    cwg = cwg_ref[...]
    cwv = cwv_ref[...]
    gate = cbg_ref[...] + bg0_ref[...] * cwg[0:1] + bg1_ref[...] * cwg[1:2] + ug * cwg[2:3]
    val = cbv_ref[...] + bv0_ref[...] * cwv[0:1] + bv1_ref[...] * cwv[1:2] + uv * cwv[2:3]
    xo_ref[...] += _dot((jax.nn.gelu(gate) * val).astype(BF16), wdn_ref[...])

    if final:
        @pl.when(j == pl.num_programs(0) - 1)
        def _():
            xo_ref[...] = _rmsnorm(xo_ref[...], gf_ref[...])


def _full_spec(shape):
    zeros = (0,) * len(shape)
    return pl.BlockSpec(tuple(shape), lambda *_: zeros)


def _sample_layer(x, rg_rows, h_all, s_all, ffn_rows, w, l, final):
    nb = x.shape[0]
    params = pltpu.CompilerParams(dimension_semantics=("arbitrary",),
                                  vmem_limit_bytes=VMEM_LIMIT)

    pre_names = ("norm1_g", "w_a", "w_glr", "w_b", "rg_conv_w", "rg_conv_b", "rg_wa", "rg_ba",
                 "rg_wx", "rg_bx", "rg_lambda", "gla_w_gate", "gla_b_gate", "w_branch_a")
    pre_out = [(nb, D_RNN), (nb, D_RNN), (nb, D_MODEL), (QK_DIM, nb), (QK_DIM, nb), (QK_DIM, nb),
               (nb, V_DIM), (nb, V_DIM), (nb, W_B_COLS)]
    rg_x, h, ya, a_t, k_t, w_t, v, okv, zb = pl.pallas_call(
        _sample_pre_kernel,
        grid=(1,),
        in_specs=[_full_spec(x.shape)]
                 + [_layer_spec(r.shape[1:], l, single=False) for r in rg_rows]
                 + [_layer_spec(h_all.shape[1:], l, single=False)]
                 + [_layer_spec(w[n].shape[1:], l) for n in pre_names],
        out_specs=[_full_spec(s) for s in pre_out],
        out_shape=[jax.ShapeDtypeStruct(s, F32) for s in pre_out],
        compiler_params=params,
        name="sample_pre",
    )(x, *rg_rows, h_all, *[w[n] for n in pre_names])

    s_blk = (None, SAMPLE_ROWS, GLA_HEADS, GLA_DK, GLA_DV)
    row_spec = pl.BlockSpec((SAMPLE_ROWS, V_DIM), lambda i: (i, 0))
    o = pl.pallas_call(
        functools.partial(_sample_read_kernel, nb=nb),
        grid=(nb // SAMPLE_ROWS,),
        in_specs=[pl.BlockSpec(s_blk, lambda i: (l, i, 0, 0, 0)), _full_spec((QK_DIM, nb)),
                  row_spec],
        out_specs=row_spec,
        out_shape=jax.ShapeDtypeStruct((nb, V_DIM), F32),
        compiler_params=params,
        name="sample_read",
    )(s_all, w_t, okv)

    tf = SAMPLE_FF_TILE
    nf = D_FF // tf

    def lcol(rows, off):
        return pl.BlockSpec((None, rows, tf), lambda j: (l, 0, off + j))

    ucol = pl.BlockSpec((nb, tf), lambda j: (0, j))
    xo, ug, uv = pl.pallas_call(
        functools.partial(_sample_post_kernel, final=final),
        grid=(nf,),
        in_specs=[_full_spec(x.shape), _full_spec(ya.shape), _full_spec(o.shape),
                  _full_spec(zb.shape),
                  _layer_spec(w["gla_norm_g"].shape[1:], l, single=False),
                  _layer_spec(w["w_branch_b"].shape[1:], l, single=False),
                  _layer_spec(w["w_out"].shape[1:], l, single=False),
                  _layer_spec(w["norm2_g"].shape[1:], l, single=False),
                  lcol(D_MODEL, 0), lcol(D_MODEL, nf),
                  lcol(FFN_CONV, 0), lcol(FFN_CONV, nf),
                  lcol(1, 0), lcol(1, nf),
                  lcol(nb, 0), lcol(nb, nf), lcol(nb, 0), lcol(nb, nf),
                  pl.BlockSpec((None, tf, D_MODEL), lambda j: (l, j, 0)),
                  _full_spec((1, D_MODEL))],
        out_specs=[_full_spec(x.shape), ucol, ucol],
        out_shape=[jax.ShapeDtypeStruct(x.shape, F32),
                   jax.ShapeDtypeStruct((nb, D_FF), F32), jax.ShapeDtypeStruct((nb, D_FF), F32)],
        scratch_shapes=[pltpu.VMEM((nb, D_MODEL), BF16)],
        compiler_params=params,
        name="sample_post",
    )(x, ya, o, zb, w["gla_norm_g"], w["w_branch_b"], w["w_out"], w["norm2_g"],
      w["w_ffn_up"], w["w_ffn_up"], w["ffn_conv_w"], w["ffn_conv_w"],
      w["ffn_conv_b"], w["ffn_conv_b"], ffn_rows[0], ffn_rows[0], ffn_rows[1], ffn_rows[1],
      w["w_ffn_down"], w["final_norm_g"])
    return xo, rg_x, h, (a_t, k_t, v), (ug, uv)


def _sample_state_update(s_all, a_t, k_t, v):
    nb = v.shape[1]
    s_blk = (None, SAMPLE_ROWS, GLA_HEADS, GLA_DK, GLA_DV)
    s_spec = pl.BlockSpec(s_blk, lambda l, i: (l, i, 0, 0, 0))
    t_spec = pl.BlockSpec((None, QK_DIM, nb), lambda l, i: (l, 0, 0))
    return pl.pallas_call(
        functools.partial(_sample_update_kernel, nb=nb),
        grid=(DEPTH, nb // SAMPLE_ROWS),
        in_specs=[s_spec, t_spec, t_spec,
                  pl.BlockSpec((None, SAMPLE_ROWS, V_DIM), lambda l, i: (l, i, 0))],
        out_specs=s_spec,
        out_shape=jax.ShapeDtypeStruct(s_all.shape, F32),
        compiler_params=pltpu.CompilerParams(
            dimension_semantics=("arbitrary", "arbitrary"), vmem_limit_bytes=VMEM_LIMIT),
        name="sample_update",
    )(s_all, a_t, k_t, v)


def _prep_weights(norm1_g, w_in, rg_conv_w, rg_conv_b, rg_wa, rg_ba, rg_wx, rg_bx, rg_lambda,
                  gla_w_gate, gla_b_gate, gla_norm_g, w_branch_a, w_branch_b, w_out, norm2_g,
                  w_ffn_up, ffn_conv_w, ffn_conv_b, w_ffn_down, final_norm_g):
    def vec(a):
        return a.reshape(DEPTH, 1, -1).astype(F32)

    def block_diag(wb):
        per = V7X_MXU_DIM // RG_BW
        w5 = wb.reshape(DEPTH, RG_BLOCKS // per, per, RG_BW, RG_BW)
        bd = jnp.einsum("lgicd,ij->lgicjd", w5, jnp.eye(per, dtype=wb.dtype))
        return bd.reshape(DEPTH, RG_BLOCKS // per, V7X_MXU_DIM, V7X_MXU_DIM).astype(BF16)

    pad_r = V7X_LANES - GLA_RANK
    return dict(
        norm1_g=vec(norm1_g),
        w_a=w_in[:, :, 0:W_A_COLS].astype(BF16),
        w_glr=jnp.pad(w_in[:, :, W_GLR_OFF:W_B_OFF], ((0, 0), (0, 0), (0, pad_r))).astype(BF16),
        w_b=w_in[:, :, W_B_OFF:W_B_OFF + W_B_COLS].astype(BF16),
        rg_conv_w=rg_conv_w.astype(F32), rg_conv_b=vec(rg_conv_b),
        rg_wa=block_diag(rg_wa), rg_ba=vec(rg_ba), rg_wx=block_diag(rg_wx), rg_bx=vec(rg_bx),
        rg_lambda=vec(rg_lambda),
        gla_w_gate=jnp.pad(gla_w_gate, ((0, 0), (0, pad_r), (0, 0))).astype(BF16),
        gla_b_gate=vec(gla_b_gate), gla_norm_g=vec(gla_norm_g),
        w_branch_a=w_branch_a.astype(BF16), w_branch_b=w_branch_b.astype(BF16),
        w_out=w_out.astype(BF16), norm2_g=vec(norm2_g),
        w_ffn_up=w_ffn_up.astype(BF16), ffn_conv_w=ffn_conv_w.astype(F32),
        ffn_conv_b=vec(ffn_conv_b), w_ffn_down=w_ffn_down.astype(BF16),
        final_norm_g=final_norm_g.reshape(1, D_MODEL).astype(F32),
    )


def kernel(x_prompt, x_sample, state_rg_conv, state_rg_h, state_gla, state_ffn_conv, meta_tokens, norm1_g, w_in, rg_conv_w, rg_conv_b, rg_wa, rg_ba, rg_wx, rg_bx, rg_lambda, gla_w_gate, gla_b_gate, gla_norm_g, w_branch_a, w_branch_b, w_out, norm2_g, w_ffn_up, ffn_conv_w, ffn_conv_b, w_ffn_down, final_norm_g):
    w = _prep_weights(norm1_g, w_in, rg_conv_w, rg_conv_b, rg_wa, rg_ba, rg_wx, rg_bx, rg_lambda,
                      gla_w_gate, gla_b_gate, gla_norm_g, w_branch_a, w_branch_b, w_out, norm2_g,
                      w_ffn_up, ffn_conv_w, ffn_conv_b, w_ffn_down, final_norm_g)
    bsz = x_prompt.shape[0]

    zero_states = [(jnp.zeros((1, RG_CONV - 1, D_RNN), F32), jnp.zeros((1, 1, D_RNN), F32),
                    jnp.zeros((1, GLA_HEADS, GLA_DK, GLA_DV), F32),
                    jnp.zeros((1, FFN_CONV - 1, 2 * D_FF), F32))] * DEPTH
    _, meta_states = _trunk_seq(meta_tokens.astype(F32)[None], zero_states, w,
                                tt=N_META, ffn_tt=N_META, chunk=N_META)

    y_prompt, p_states = _trunk_seq(x_prompt, meta_states, w, tt=SEQ_TILE, ffn_tt=FFN_SEQ_TILE,
                                    chunk=GLA_CHUNK)
    rg_conv_prompt = jnp.stack([s[0] for s in p_states])
    rg_h_prompt = jnp.stack([s[1].reshape(bsz, D_RNN) for s in p_states])
    gla_prompt = jnp.stack([s[2] for s in p_states])
    ffn_conv_prompt = jnp.stack([s[3] for s in p_states])

    nb = x_sample.shape[0]
    xs = x_sample.reshape(nb, D_MODEL)
    rg_rows = [state_rg_conv[:, :, j] for j in range(RG_CONV - 1)]
    ffn_rows = [state_ffn_conv[:, :, j] for j in range(FFN_CONV - 1)]
    rg_xs, hs, akv, us = [], [], [], []
    for l in range(DEPTH):
        xs, rg_x, h, akv_l, u_l = _sample_layer(xs, rg_rows, state_rg_h, state_gla, ffn_rows,
                                                w, l, final=(l == DEPTH - 1))
        rg_xs.append(rg_x)
        hs.append(h)
        akv.append(akv_l)
        us.append(jnp.concatenate(u_l, axis=1))
    gla_sample = _sample_state_update(state_gla, *[jnp.stack(t) for t in zip(*akv)])
    rg_conv_sample = jnp.stack(rg_rows[1:] + [jnp.stack(rg_xs)], axis=2)
    ffn_conv_sample = jnp.stack(ffn_rows[1:] + [jnp.stack(us)], axis=2)
    y_sample = xs.reshape(x_sample.shape)
    return (y_prompt, y_sample, rg_conv_prompt, rg_h_prompt, gla_prompt, ffn_conv_prompt,
            rg_conv_sample, jnp.stack(hs), gla_sample, ffn_conv_sample)
```

```python
import functools

import jax
import jax.numpy as jnp
from jax import lax
from jax.experimental import pallas as pl
from jax.experimental.pallas import tpu as pltpu

F32 = jnp.float32
BF16 = jnp.bfloat16

D_MODEL = 1024
DEPTH = 4
D_RNN = D_MODEL
RG_BLOCKS = 16
RG_BW = D_RNN // RG_BLOCKS
RG_CONV = 4
RG_C = 8.0
GLA_HEADS = 4
GLA_DK = 128
GLA_DV = 256
QK_DIM = GLA_HEADS * GLA_DK
V_DIM = GLA_HEADS * GLA_DV
GLA_RANK = 16
GLA_TAU = 16.0
GLA_CHUNK = 64
D_FF = 3 * D_MODEL
FFN_CONV = 3
N_META = 16
EPS = 1e-6

V7X_MXU_DIM = 256
V7X_LANES = 128
V7X_SUBLANES = 8

W_A_COLS = 2 * D_RNN + 2 * QK_DIM + V_DIM
W_GLR_OFF = W_A_COLS
W_B_OFF = W_A_COLS + GLA_RANK
W_B_COLS = V_DIM + 2 * D_MODEL
Q_OFF, K_OFF, V_OFF = 2 * D_RNN, 2 * D_RNN + QK_DIM, 2 * D_RNN + 2 * QK_DIM

SEQ_TILE = 512
FFN_SEQ_TILE = 512
FFN_COL_TILE = 1024
SAMPLE_ROWS = 8
SAMPLE_FF_TILE = 512
CAST_COL_TILE = 512
VMEM_LIMIT = 56 * 1024 * 1024


def _dot(a, b):
    return jnp.dot(a, b, preferred_element_type=F32)


def _rmsnorm(x, g):
    ms = jnp.mean(x * x, axis=-1, keepdims=True)
    return x * lax.rsqrt(ms + EPS) * g


def _softplus(x):
    return jnp.maximum(x, 0.0) + jnp.log1p(jnp.exp(-jnp.abs(x)))


def _block_diag_dot(xb, w_ref):
    n = D_RNN // V7X_MXU_DIM
    return jnp.concatenate(
        [_dot(xb[:, i * V7X_MXU_DIM:(i + 1) * V7X_MXU_DIM], w_ref[i]) for i in range(n)], axis=1)


def _rglru_coeffs(xc, rwa_ref, rba_ref, rwx_ref, rbx_ref, lam_ref):
    xcb = xc.astype(BF16)
    r = jax.nn.sigmoid(_block_diag_dot(xcb, rwa_ref) + rba_ref[...])
    i = jax.nn.sigmoid(_block_diag_dot(xcb, rwx_ref) + rbx_ref[...])
    log_a = (-RG_C) * r * _softplus(-lam_ref[...])
    a = jnp.exp(log_a)
    y = 1.0 - a * a
    b = jnp.where(y > 0.0, y * lax.rsqrt(y), 0.0) * (i * xc)
    return a, b


def _head_norm_gate(o, gng, g_out):
    parts = []
    for hh in range(GLA_HEADS):
        sl = slice(hh * GLA_DV, (hh + 1) * GLA_DV)
        oh = o[:, sl]
        ms = jnp.mean(oh * oh, axis=-1, keepdims=True)
        parts.append(oh * lax.rsqrt(ms + EPS) * gng[:, sl])
    on = jnp.concatenate(parts, axis=1)
    return on * (g_out * jax.nn.sigmoid(g_out))


def _delayed_rows(prev, u, s):
    ext = jnp.concatenate([prev, u], axis=0)
    return pltpu.roll(ext, s, axis=0)[V7X_SUBLANES:]


def _causal_conv(u, prev, cw, cb, hist):
    y = cb + _delayed_rows(prev, u, hist) * cw[0:1]
    for j in range(1, hist):
        y = y + _delayed_rows(prev, u, hist - j) * cw[j:j + 1]
    return y + u * cw[hist:hist + 1]


def _scan_affine(a, b, h_init, tt):
    width = a.shape[1]
    n_tiles = tt // V7X_SUBLANES
    a = a.reshape(n_tiles, V7X_SUBLANES, width)
    b = b.reshape(n_tiles, V7X_SUBLANES, width)
    sub = lax.broadcasted_iota(jnp.int32, a.shape, 1)
    s = 1
    while s < V7X_SUBLANES:
        keep = sub >= s
        b = jnp.where(keep, a * pltpu.roll(b, s, axis=1) + b, b)
        a = jnp.where(keep, a * pltpu.roll(a, s, axis=1), a)
        s *= 2
    tiles = []
    carry = h_init
    for g in range(n_tiles):
        h = b[g] + a[g] * carry
        carry = h[V7X_SUBLANES - 1:V7X_SUBLANES, :]
        tiles.append(h)
    return jnp.concatenate(tiles, axis=0)


def _gla_decay(glr, wgate_ref, bgate_ref, tt, chunk):
    zg = _dot(glr.astype(BF16), wgate_ref[...]) + bgate_ref[...]
    la = -_softplus(-zg) * (1.0 / GLA_TAU)
    blk = min(tt, V7X_MXU_DIM)
    blocks = [slice(r0, r0 + blk) for r0 in range(0, tt, blk)]
    shift = chunk.bit_length() - 1
    r_i = lax.broadcasted_iota(jnp.int32, (blk, blk), 0)
    c_i = lax.broadcasted_iota(jnp.int32, (blk, blk), 1)
    causal = ((r_i >> shift) == (c_i >> shift)) & (c_i <= r_i)
    cum = jnp.where(causal, 1.0, 0.0).astype(BF16)
    la_hi = la.astype(BF16)
    la_lo = (la - la_hi.astype(F32)).astype(BF16)
    bc = jnp.concatenate([_dot(cum, la_hi[rb]) + _dot(cum, la_lo[rb]) for rb in blocks], axis=0)
    return bc, causal, blocks


def _gla_heads(qk, v, bc, causal, blocks, s_ref, o_ref, chunk):
    blk = blocks[0].stop
    qd_all = (qk[:, 0:QK_DIM] * (GLA_DK ** -0.5) * jnp.exp(bc)).astype(BF16)
    kd_all = (qk[:, QK_DIM:2 * QK_DIM] * jnp.exp(-bc)).astype(BF16)
    for hh in range(GLA_HEADS):
        ks = slice(hh * GLA_DK, (hh + 1) * GLA_DK)
        vs = slice(hh * GLA_DV, (hh + 1) * GLA_DV)
        for rb in blocks:
            att = lax.dot_general(qd_all[rb, ks], kd_all[rb, ks], (((1,), (1,)), ((), ())),
                                  preferred_element_type=F32)
            o_intra = _dot(jnp.where(causal, att, 0.0).astype(BF16), v[rb, vs])
            for c0 in range(0, blk, chunk):
                rs = slice(rb.start + c0, rb.start + c0 + chunk)
                g = bc[rs, ks]
                gl = g[chunk - 1:chunk, :]
                ke = (qk[rs, QK_DIM + hh * GLA_DK:QK_DIM + (hh + 1) * GLA_DK]
                      * jnp.exp(gl - g)).astype(BF16)
                st = s_ref[hh]
                o_ref[rs, vs] = o_intra[c0:c0 + chunk] + _dot(qd_all[rs, ks], st.astype(BF16))
                ds = lax.dot_general(ke, v[rs, vs], (((0,), (0,)), ((), ())),
                                     preferred_element_type=F32)
                dec = jnp.broadcast_to(jnp.exp(gl), (GLA_DK, GLA_DK)).T
                s_ref[hh] = jnp.concatenate([dec] * (GLA_DV // GLA_DK), axis=1) * st + ds


def _mixer_tail(x, zb, ya, o, gng_ref, wbrb_ref, wout_ref):
    on = _head_norm_gate(o, gng_ref[...], zb[:, 0:V_DIM])
    yb = _dot(on.astype(BF16), wbrb_ref[...])
    merged = (jax.nn.sigmoid(zb[:, V_DIM:V_DIM + D_MODEL]) * ya
              + jax.nn.sigmoid(zb[:, V_DIM + D_MODEL:V_DIM + 2 * D_MODEL]) * yb)
    return x + _dot(merged.astype(BF16), wout_ref[...])


def _mixer_seq_kernel(x_ref, rgbuf0_ref, h0_ref, s0_ref, g1_ref, wa_ref, wglr_ref, wb_ref,
                      cw_ref, cb_ref, rwa_ref, rba_ref, rwx_ref, rbx_ref, lam_ref,
                      wgate_ref, bgate_ref, gng_ref, wbra_ref, wbrb_ref, wout_ref,
                      xo_ref, rgbuf_ref, h_ref, s_ref,
                      hist_ref, o_ref, *, tt, chunk):
    hist = RG_CONV - 1

    @pl.when(pl.program_id(1) == 0)
    def _():
        hist_ref[...] = jnp.zeros(hist_ref.shape, F32)
        hist_ref[V7X_SUBLANES - hist:V7X_SUBLANES, :] = rgbuf0_ref[...]
        h_ref[...] = h0_ref[...]
        s_ref[...] = s0_ref[...]

    x = x_ref[...]
    xn = _rmsnorm(x, g1_ref[...]).astype(BF16)

    rg_x = _dot(xn, wa_ref[:, 0:D_RNN])
    glr = _dot(xn, wglr_ref[...])
    xc = _causal_conv(rg_x, hist_ref[...], cw_ref[...], cb_ref[...], hist)
    hist_ref[...] = rg_x[tt - V7X_SUBLANES:tt, :]
    rgbuf_ref[...] = hist_ref[V7X_SUBLANES - hist:V7X_SUBLANES, :]
    qk = _dot(xn, wa_ref[:, Q_OFF:V_OFF])
    a, b = _rglru_coeffs(xc, rwa_ref, rba_ref, rwx_ref, rbx_ref, lam_ref)
    rg_y = _dot(xn, wa_ref[:, D_RNN:2 * D_RNN])
    v = _dot(xn, wa_ref[:, V_OFF:V_OFF + V_DIM]).astype(BF16)
    h = _scan_affine(a, b, h_ref[...], tt)
    h_ref[...] = h[tt - 1:tt, :]
    zb = _dot(xn, wb_ref[...])
    ya = _dot((h * jax.nn.gelu(rg_y)).astype(BF16), wbra_ref[...])

    bc, causal, blocks = _gla_decay(glr, wgate_ref, bgate_ref, tt, chunk)
    _gla_heads(qk, v, bc, causal, blocks, s_ref, o_ref, chunk)
    xo_ref[...] = _mixer_tail(x, zb, ya, o_ref[...], gng_ref, wbrb_ref, wout_ref)


def _ffn_seq_kernel(x_ref, buf0_ref, g2_ref, wup_ref, cw_ref, cb_ref, wdn_ref, gf_ref,
                    xo_ref, buf_ref, hist_ref, *, tt, cf, final):
    hist = FFN_CONV - 1

    @pl.when(pl.program_id(1) == 0)
    def _():
        hist_ref[...] = jnp.zeros(hist_ref.shape, F32)
        hist_ref[V7X_SUBLANES - hist:V7X_SUBLANES, :] = buf0_ref[...]

    x = x_ref[...]
    xn = _rmsnorm(x, g2_ref[...]).astype(BF16)
    def up_proj(j):
        return [(slice(off, off + cf), _dot(xn, wup_ref[:, off:off + cf]))
                for off in (j * cf, D_FF + j * cf)]

    acc = x
    n_chunks = D_FF // cf
    ups = up_proj(0)
    for j in range(n_chunks):
        nxt = up_proj(j + 1) if j + 1 < n_chunks else None
        halves = []
        for cs, u in ups:
            halves.append(_causal_conv(u, hist_ref[:, cs], cw_ref[:, cs], cb_ref[:, cs], hist))
            hist_ref[:, cs] = u[tt - V7X_SUBLANES:tt, :]
        hmid = (jax.nn.gelu(halves[0]) * halves[1]).astype(BF16)
        acc = acc + _dot(hmid, wdn_ref[j * cf:(j + 1) * cf, :])
        ups = nxt
    buf_ref[...] = hist_ref[V7X_SUBLANES - hist:V7X_SUBLANES, :]
    xo_ref[...] = _rmsnorm(acc, gf_ref[...]) if final else acc


def _layer_spec(tail, l, single=True):
    zeros = (0,) * len(tail)
    kw = dict(pipeline_mode=pl.Buffered(1)) if single else {}
    return pl.BlockSpec((None,) + tuple(tail), lambda *_: (l,) + zeros, **kw)


def _mixer_seq(x, rgbuf0, h0, s0, w, l, tt, chunk):
    bsz, t_len, _ = x.shape
    bi = 1 if rgbuf0.shape[0] == bsz else 0

    def state_spec(tail):
        zeros = (0,) * len(tail)
        return pl.BlockSpec((None,) + tuple(tail), lambda b, t: (b * bi,) + zeros)

    def out_state_spec(tail):
        zeros = (0,) * len(tail)
        return pl.BlockSpec((None,) + tuple(tail), lambda b, t: (b,) + zeros)

    x_spec = pl.BlockSpec((None, tt, D_MODEL), lambda b, t: (b, t, 0))
    hist = RG_CONV - 1
    names = ("norm1_g", "w_a", "w_glr", "w_b", "rg_conv_w", "rg_conv_b", "rg_wa", "rg_ba",
             "rg_wx", "rg_bx", "rg_lambda", "gla_w_gate", "gla_b_gate", "gla_norm_g",
             "w_branch_a", "w_branch_b", "w_out")
    return pl.pallas_call(
        functools.partial(_mixer_seq_kernel, tt=tt, chunk=chunk),
        grid=(bsz, t_len // tt),
        in_specs=[x_spec, state_spec((hist, D_RNN)), state_spec((1, D_RNN)),
                  state_spec((GLA_HEADS, GLA_DK, GLA_DV))]
                 + [_layer_spec(w[n].shape[1:], l) for n in names],
        out_specs=[x_spec, out_state_spec((hist, D_RNN)), out_state_spec((1, D_RNN)),
                   out_state_spec((GLA_HEADS, GLA_DK, GLA_DV))],
        out_shape=[jax.ShapeDtypeStruct(x.shape, F32),
                   jax.ShapeDtypeStruct((bsz, hist, D_RNN), F32),
                   jax.ShapeDtypeStruct((bsz, 1, D_RNN), F32),
                   jax.ShapeDtypeStruct((bsz, GLA_HEADS, GLA_DK, GLA_DV), F32)],
        scratch_shapes=[pltpu.VMEM((V7X_SUBLANES, D_RNN), F32),
                        pltpu.VMEM((tt, V_DIM), F32)],
        compiler_params=pltpu.CompilerParams(
            dimension_semantics=("arbitrary", "arbitrary"), vmem_limit_bytes=VMEM_LIMIT),
        name="mixer_seq",
    )(x, rgbuf0, h0, s0, *[w[n] for n in names])


def _ffn_seq(x, buf0, w, l, tt, final):
    bsz, t_len, _ = x.shape
    bi = 1 if buf0.shape[0] == bsz else 0
    hist = FFN_CONV - 1
    cf = FFN_COL_TILE
    x_spec = pl.BlockSpec((None, tt, D_MODEL), lambda b, t: (b, t, 0))
    names = ("norm2_g", "w_ffn_up", "ffn_conv_w", "ffn_conv_b", "w_ffn_down")
    return pl.pallas_call(
        functools.partial(_ffn_seq_kernel, tt=tt, cf=cf, final=final),
        grid=(bsz, t_len // tt),
        in_specs=[x_spec, pl.BlockSpec((None, hist, 2 * D_FF), lambda b, t: (b * bi, 0, 0))]
                 + [_layer_spec(w[n].shape[1:], l) for n in names]
                 + [pl.BlockSpec((1, D_MODEL), lambda b, t: (0, 0))],
        out_specs=[x_spec, pl.BlockSpec((None, hist, 2 * D_FF), lambda b, t: (b, 0, 0))],
        out_shape=[jax.ShapeDtypeStruct(x.shape, F32),
                   jax.ShapeDtypeStruct((bsz, hist, 2 * D_FF), F32)],
        scratch_shapes=[pltpu.VMEM((V7X_SUBLANES, 2 * D_FF), F32)],
        compiler_params=pltpu.CompilerParams(
            dimension_semantics=("arbitrary", "arbitrary"), vmem_limit_bytes=VMEM_LIMIT),
        name="ffn_seq",
    )(x, buf0, *[w[n] for n in names], w["final_norm_g"])


def _trunk_seq(x, states, w, tt, ffn_tt, chunk):
    outs = []
    for l in range(DEPTH):
        rgbuf0, h0, s0, fbuf0 = states[l]
        x, rgbuf, h, s = _mixer_seq(x, rgbuf0, h0, s0, w, l, tt, chunk)
        x, fbuf = _ffn_seq(x, fbuf0, w, l, ffn_tt, final=(l == DEPTH - 1))
        outs.append((rgbuf, h, s, fbuf))
    return x, outs


def _sample_pre_kernel(x_ref, rb0_ref, rb1_ref, rb2_ref, h0_ref, g1_ref, wa_ref, wglr_ref, wb_ref,
                       cw_ref, cb_ref, rwa_ref, rba_ref, rwx_ref, rbx_ref, lam_ref,
                       wgate_ref, bgate_ref, wbra_ref,
                       rgx_ref, h_ref, ya_ref, at_ref, kt_ref, wt_ref, v_ref, okv_ref, zb_ref):
    hist = RG_CONV - 1
    rb_refs = (rb0_ref, rb1_ref, rb2_ref)
    xn = _rmsnorm(x_ref[...], g1_ref[...]).astype(BF16)
    za = _dot(xn, wa_ref[...])
    glr = _dot(xn, wglr_ref[...])
    zb_ref[...] = _dot(xn, wb_ref[...])

    rg_x = za[:, 0:D_RNN]
    rgx_ref[...] = rg_x
    cw = cw_ref[...]
    xc = cb_ref[...] + rb_refs[0][...] * cw[0:1]
    for j in range(1, hist):
        xc = xc + rb_refs[j][...] * cw[j:j + 1]
    xc = xc + rg_x * cw[hist:hist + 1]
    a, b = _rglru_coeffs(xc, rwa_ref, rba_ref, rwx_ref, rbx_ref, lam_ref)
    h = a * h0_ref[...] + b
    h_ref[...] = h
    ya_ref[...] = _dot((h * jax.nn.gelu(za[:, D_RNN:2 * D_RNN])).astype(BF16), wbra_ref[...])

    zg = _dot(glr.astype(BF16), wgate_ref[...]) + bgate_ref[...]
    alpha = jnp.exp(-_softplus(-zg) * (1.0 / GLA_TAU))
    q = za[:, Q_OFF:Q_OFF + QK_DIM] * (GLA_DK ** -0.5)
    k = za[:, K_OFF:K_OFF + QK_DIM]
    v = za[:, V_OFF:V_OFF + V_DIM]
    at_ref[...] = alpha.T
    kt_ref[...] = k.T
    wt_ref[...] = (q * alpha).T
    v_ref[...] = v
    qk = q * k
    okv_ref[...] = jnp.concatenate(
        [jnp.sum(qk[:, hh * GLA_DK:(hh + 1) * GLA_DK], axis=-1, keepdims=True)
         * v[:, hh * GLA_DV:(hh + 1) * GLA_DV] for hh in range(GLA_HEADS)], axis=1)


def _row_onehot(nb, row):
    rows = lax.broadcasted_iota(jnp.int32, (nb, GLA_DV), 0)
    return jnp.where(rows == row, 1.0, 0.0).astype(BF16)


def _sample_read_kernel(s0_ref, wt_ref, okv_ref, o_ref, *, nb):
    i = pl.program_id(0)
    wb = wt_ref[...].astype(BF16)
    for j in range(SAMPLE_ROWS):
        cw = _dot(wb, _row_onehot(nb, i * SAMPLE_ROWS + j))
        for hh in range(GLA_HEADS):
            ks = slice(hh * GLA_DK, (hh + 1) * GLA_DK)
            vs = slice(hh * GLA_DV, (hh + 1) * GLA_DV)
            o_ref[j:j + 1, vs] = (jnp.sum(cw[ks] * s0_ref[j, hh], axis=0, keepdims=True)
                                  + okv_ref[j:j + 1, vs])


def _sample_update_kernel(s0_ref, at_ref, kt_ref, v_ref, s_ref, *, nb):
    i = pl.program_id(1)
    al = at_ref[...]
    a_hi = al.astype(BF16)
    r1 = al - a_hi.astype(F32)
    a_mid = r1.astype(BF16)
    a_lo = (r1 - a_mid.astype(F32)).astype(BF16)
    kb = kt_ref[...].astype(BF16)
    for j in range(SAMPLE_ROWS):
        e = _row_onehot(nb, i * SAMPLE_ROWS + j)
        ca = _dot(a_hi, e) + _dot(a_mid, e) + _dot(a_lo, e)
        ck = _dot(kb, e)
        for hh in range(GLA_HEADS):
            ks = slice(hh * GLA_DK, (hh + 1) * GLA_DK)
            vs = slice(hh * GLA_DV, (hh + 1) * GLA_DV)
            s_ref[j, hh] = ca[ks] * s0_ref[j, hh] + ck[ks] * v_ref[j:j + 1, vs]


def _sample_post_kernel(x_ref, ya_ref, o_ref, zb_ref, gng_ref, wbrb_ref, wout_ref, g2_ref,
                        wupg_ref, wupv_ref, cwg_ref, cwv_ref, cbg_ref, cbv_ref,
                        bg0_ref, bv0_ref, bg1_ref, bv1_ref, wdn_ref, gf_ref,
                        xo_ref, ug_ref, uv_ref, xn_ref, *, final):
    j = pl.program_id(0)

    @pl.when(j == 0)
    def _():
        zb = zb_ref[...]
        on = _head_norm_gate(o_ref[...], gng_ref[...], zb[:, 0:V_DIM])
        yb = _dot(on.astype(BF16), wbrb_ref[...])
        merged = (jax.nn.sigmoid(zb[:, V_DIM:V_DIM + D_MODEL]) * ya_ref[...]
                  + jax.nn.sigmoid(zb[:, V_DIM + D_MODEL:V_DIM + 2 * D_MODEL]) * yb)
        x1 = x_ref[...] + _dot(merged.astype(BF16), wout_ref[...])
        xo_ref[...] = x1
        xn_ref[...] = _rmsnorm(x1, g2_ref[...]).astype(BF16)

    xn = xn_ref[...]
    ug = _dot(xn, wupg_ref[...])
    uv = _dot(xn, wupv_ref[...])
    ug_ref[...] = ug
    uv_ref[...] = uv
    cwg = cwg_ref[...]
    cwv = cwv_ref[...]
    gate = cbg_ref[...] + bg0_ref[...] * cwg[0:1] + bg1_ref[...] * cwg[1:2] + ug * cwg[2:3]
    val = cbv_ref[...] + bv0_ref[...] * cwv[0:1] + bv1_ref[...] * cwv[1:2] + uv * cwv[2:3]
    xo_ref[...] += _dot((jax.nn.gelu(gate) * val).astype(BF16), wdn_ref[...])

    if final:
        @pl.when(j == pl.num_programs(0) - 1)
        def _():
            xo_ref[...] = _rmsnorm(xo_ref[...], gf_ref[...])


def _full_spec(shape):
    zeros = (0,) * len(shape)
    return pl.BlockSpec(tuple(shape), lambda *_: zeros)


def _sample_layer(x, rg_rows, h_all, s_all, ffn_rows, w, l, final):
    nb = x.shape[0]
    params = pltpu.CompilerParams(dimension_semantics=("arbitrary",),
                                  vmem_limit_bytes=VMEM_LIMIT)

    pre_names = ("norm1_g", "w_a", "w_glr", "w_b", "rg_conv_w", "rg_conv_b", "rg_wa", "rg_ba",
                 "rg_wx", "rg_bx", "rg_lambda", "gla_w_gate", "gla_b_gate", "w_branch_a")
    pre_out = [(nb, D_RNN), (nb, D_RNN), (nb, D_MODEL), (QK_DIM, nb), (QK_DIM, nb), (QK_DIM, nb),
               (nb, V_DIM), (nb, V_DIM), (nb, W_B_COLS)]
    rg_x, h, ya, a_t, k_t, w_t, v, okv, zb = pl.pallas_call(
        _sample_pre_kernel,
        grid=(1,),
        in_specs=[_full_spec(x.shape)]
                 + [_layer_spec(r.shape[1:], l, single=False) for r in rg_rows]
                 + [_layer_spec(h_all.shape[1:], l, single=False)]
                 + [_layer_spec(w[n].shape[1:], l) for n in pre_names],
        out_specs=[_full_spec(s) for s in pre_out],
        out_shape=[jax.ShapeDtypeStruct(s, F32) for s in pre_out],
        compiler_params=params,
        name="sample_pre",
    )(x, *rg_rows, h_all, *[w[n] for n in pre_names])

    s_blk = (None, SAMPLE_ROWS, GLA_HEADS, GLA_DK, GLA_DV)
    row_spec = pl.BlockSpec((SAMPLE_ROWS, V_DIM), lambda i: (i, 0))
    o = pl.pallas_call(
        functools.partial(_sample_read_kernel, nb=nb),
        grid=(nb // SAMPLE_ROWS,),
        in_specs=[pl.BlockSpec(s_blk, lambda i: (l, i, 0, 0, 0)), _full_spec((QK_DIM, nb)),
                  row_spec],
        out_specs=row_spec,
        out_shape=jax.ShapeDtypeStruct((nb, V_DIM), F32),
        compiler_params=params,
        name="sample_read",
    )(s_all, w_t, okv)

    tf = SAMPLE_FF_TILE
    nf = D_FF // tf

    def lcol(rows, off):
        return pl.BlockSpec((None, rows, tf), lambda j: (l, 0, off + j))

    ucol = pl.BlockSpec((nb, tf), lambda j: (0, j))
    xo, ug, uv = pl.pallas_call(
        functools.partial(_sample_post_kernel, final=final),
        grid=(nf,),
        in_specs=[_full_spec(x.shape), _full_spec(ya.shape), _full_spec(o.shape),
                  _full_spec(zb.shape),
                  _layer_spec(w["gla_norm_g"].shape[1:], l, single=False),
                  _layer_spec(w["w_branch_b"].shape[1:], l, single=False),
                  _layer_spec(w["w_out"].shape[1:], l, single=False),
                  _layer_spec(w["norm2_g"].shape[1:], l, single=False),
                  lcol(D_MODEL, 0), lcol(D_MODEL, nf),
                  lcol(FFN_CONV, 0), lcol(FFN_CONV, nf),
                  lcol(1, 0), lcol(1, nf),
                  lcol(nb, 0), lcol(nb, nf), lcol(nb, 0), lcol(nb, nf),
                  pl.BlockSpec((None, tf, D_MODEL), lambda j: (l, j, 0)),
                  _full_spec((1, D_MODEL))],
        out_specs=[_full_spec(x.shape), ucol, ucol],
        out_shape=[jax.ShapeDtypeStruct(x.shape, F32),
                   jax.ShapeDtypeStruct((nb, D_FF), F32), jax.ShapeDtypeStruct((nb, D_FF), F32)],
        scratch_shapes=[pltpu.VMEM((nb, D_MODEL), BF16)],
        compiler_params=params,
        name="sample_post",
    )(x, ya, o, zb, w["gla_norm_g"], w["w_branch_b"], w["w_out"], w["norm2_g"],
      w["w_ffn_up"], w["w_ffn_up"], w["ffn_conv_w"], w["ffn_conv_w"],
      w["ffn_conv_b"], w["ffn_conv_b"], ffn_rows[0], ffn_rows[0], ffn_rows[1], ffn_rows[1],
      w["w_ffn_down"], w["final_norm_g"])
    return xo, rg_x, h, (a_t, k_t, v), (ug, uv)


def _sample_state_update(s_all, a_t, k_t, v):
    nb = v.shape[1]
    s_blk = (None, SAMPLE_ROWS, GLA_HEADS, GLA_DK, GLA_DV)
    s_spec = pl.BlockSpec(s_blk, lambda l, i: (l, i, 0, 0, 0))
    t_spec = pl.BlockSpec((None, QK_DIM, nb), lambda l, i: (l, 0, 0))
    return pl.pallas_call(
        functools.partial(_sample_update_kernel, nb=nb),
        grid=(DEPTH, nb // SAMPLE_ROWS),
        in_specs=[s_spec, t_spec, t_spec,
                  pl.BlockSpec((None, SAMPLE_ROWS, V_DIM), lambda l, i: (l, i, 0))],
        out_specs=s_spec,
        out_shape=jax.ShapeDtypeStruct(s_all.shape, F32),
        compiler_params=pltpu.CompilerParams(
            dimension_semantics=("arbitrary", "arbitrary"), vmem_limit_bytes=VMEM_LIMIT),
        name="sample_update",
    )(s_all, a_t, k_t, v)


def _cast_cols_kernel(w_ref, o_ref):
    o_ref[...] = w_ref[...].astype(o_ref.dtype)


def _cast_leading_cols(w, cols, dtype, col_tile):
    depth, rows, _ = w.shape
    spec = pl.BlockSpec((None, rows, col_tile), lambda l, j: (l, 0, j))
    return pl.pallas_call(
        _cast_cols_kernel,
        grid=(depth, cols // col_tile),
        in_specs=[spec], out_specs=spec,
        out_shape=jax.ShapeDtypeStruct((depth, rows, cols), dtype),
        compiler_params=pltpu.CompilerParams(dimension_semantics=("arbitrary", "arbitrary")),
        name="cast_cols",
    )(w)


def _prep_weights(norm1_g, w_in, rg_conv_w, rg_conv_b, rg_wa, rg_ba, rg_wx, rg_bx, rg_lambda,
                  gla_w_gate, gla_b_gate, gla_norm_g, w_branch_a, w_branch_b, w_out, norm2_g,
                  w_ffn_up, ffn_conv_w, ffn_conv_b, w_ffn_down, final_norm_g):
    def vec(a):
        return a.reshape(DEPTH, 1, -1).astype(F32)

    def block_diag(wb):
        per = V7X_MXU_DIM // RG_BW
        w5 = wb.reshape(DEPTH, RG_BLOCKS // per, per, RG_BW, RG_BW)
        bd = jnp.einsum("lgicd,ij->lgicjd", w5, jnp.eye(per, dtype=wb.dtype))
        return bd.reshape(DEPTH, RG_BLOCKS // per, V7X_MXU_DIM, V7X_MXU_DIM).astype(BF16)

    pad_r = V7X_LANES - GLA_RANK
    return dict(
        norm1_g=vec(norm1_g),
        w_a=_cast_leading_cols(w_in, W_A_COLS, BF16, CAST_COL_TILE),
        w_glr=jnp.pad(w_in[:, :, W_GLR_OFF:W_B_OFF], ((0, 0), (0, 0), (0, pad_r))).astype(BF16),
        w_b=w_in[:, :, W_B_OFF:W_B_OFF + W_B_COLS].astype(BF16),
        rg_conv_w=rg_conv_w.astype(F32), rg_conv_b=vec(rg_conv_b),
        rg_wa=block_diag(rg_wa), rg_ba=vec(rg_ba), rg_wx=block_diag(rg_wx), rg_bx=vec(rg_bx),
        rg_lambda=vec(rg_lambda),
        gla_w_gate=jnp.pad(gla_w_gate, ((0, 0), (0, pad_r), (0, 0))).astype(BF16),
        gla_b_gate=vec(gla_b_gate), gla_norm_g=vec(gla_norm_g),
        w_branch_a=w_branch_a.astype(BF16), w_branch_b=w_branch_b.astype(BF16),
        w_out=w_out.astype(BF16), norm2_g=vec(norm2_g),
        w_ffn_up=w_ffn_up.astype(BF16), ffn_conv_w=ffn_conv_w.astype(F32),
        ffn_conv_b=vec(ffn_conv_b), w_ffn_down=w_ffn_down.astype(BF16),
        final_norm_g=final_norm_g.reshape(1, D_MODEL).astype(F32),
    )


def kernel(x_prompt, x_sample, state_rg_conv, state_rg_h, state_gla, state_ffn_conv, meta_tokens, norm1_g, w_in, rg_conv_w, rg_conv_b, rg_wa, rg_ba, rg_wx, rg_bx, rg_lambda, gla_w_gate, gla_b_gate, gla_norm_g, w_branch_a, w_branch_b, w_out, norm2_g, w_ffn_up, ffn_conv_w, ffn_conv_b, w_ffn_down, final_norm_g):
    w = _prep_weights(norm1_g, w_in, rg_conv_w, rg_conv_b, rg_wa, rg_ba, rg_wx, rg_bx, rg_lambda,
                      gla_w_gate, gla_b_gate, gla_norm_g, w_branch_a, w_branch_b, w_out, norm2_g,
                      w_ffn_up, ffn_conv_w, ffn_conv_b, w_ffn_down, final_norm_g)
    bsz = x_prompt.shape[0]

    zero_states = [(jnp.zeros((1, RG_CONV - 1, D_RNN), F32), jnp.zeros((1, 1, D_RNN), F32),
                    jnp.zeros((1, GLA_HEADS, GLA_DK, GLA_DV), F32),
                    jnp.zeros((1, FFN_CONV - 1, 2 * D_FF), F32))] * DEPTH
    _, meta_states = _trunk_seq(meta_tokens.astype(F32)[None], zero_states, w,
                                tt=N_META, ffn_tt=N_META, chunk=N_META)

    y_prompt, p_states = _trunk_seq(x_prompt, meta_states, w, tt=SEQ_TILE, ffn_tt=FFN_SEQ_TILE,
                                    chunk=GLA_CHUNK)
    rg_conv_prompt = jnp.stack([s[0] for s in p_states])
    rg_h_prompt = jnp.stack([s[1].reshape(bsz, D_RNN) for s in p_states])
    gla_prompt = jnp.stack([s[2] for s in p_states])
    ffn_conv_prompt = jnp.stack([s[3] for s in p_states])

    nb = x_sample.shape[0]
    xs = x_sample.reshape(nb, D_MODEL)
    rg_rows = [state_rg_conv[:, :, j] for j in range(RG_CONV - 1)]
    ffn_rows = [state_ffn_conv[:, :, j] for j in range(FFN_CONV - 1)]
    rg_xs, hs, akv, us = [], [], [], []
    for l in range(DEPTH):
        xs, rg_x, h, akv_l, u_l = _sample_layer(xs, rg_rows, state_rg_h, state_gla, ffn_rows,
                                                w, l, final=(l == DEPTH - 1))
        rg_xs.append(rg_x)
        hs.append(h)
        akv.append(akv_l)
        us.append(jnp.concatenate(u_l, axis=1))
    gla_sample = _sample_state_update(state_gla, *[jnp.stack(t) for t in zip(*akv)])
    rg_conv_sample = jnp.stack(rg_rows[1:] + [jnp.stack(rg_xs)], axis=2)
    ffn_conv_sample = jnp.stack(ffn_rows[1:] + [jnp.stack(us)], axis=2)
    y_sample = xs.reshape(x_sample.shape)
    return (y_prompt, y_sample, rg_conv_prompt, rg_h_prompt, gla_prompt, ffn_conv_prompt,
            rg_conv_sample, jnp.stack(hs), gla_sample, ffn_conv_sample)
```

```python
import functools

import jax
import jax.numpy as jnp
from jax import lax
from jax.experimental import pallas as pl
from jax.experimental.pallas import tpu as pltpu

F32 = jnp.float32
BF16 = jnp.bfloat16

D_MODEL = 1024
DEPTH = 4
D_RNN = D_MODEL
RG_BLOCKS = 16
RG_BW = D_RNN // RG_BLOCKS
RG_CONV = 4
RG_C = 8.0
GLA_HEADS = 4
GLA_DK = 128
GLA_DV = 256
QK_DIM = GLA_HEADS * GLA_DK
V_DIM = GLA_HEADS * GLA_DV
GLA_RANK = 16
GLA_TAU = 16.0
GLA_CHUNK = 64
D_FF = 3 * D_MODEL
FFN_CONV = 3
N_META = 16
EPS = 1e-6

V7X_MXU_DIM = 256
V7X_LANES = 128
V7X_SUBLANES = 8

W_A_COLS = 2 * D_RNN + 2 * QK_DIM + V_DIM
W_GLR_OFF = W_A_COLS
W_B_OFF = W_A_COLS + GLA_RANK
W_B_COLS = V_DIM + 2 * D_MODEL
Q_OFF, K_OFF, V_OFF = 2 * D_RNN, 2 * D_RNN + QK_DIM, 2 * D_RNN + 2 * QK_DIM

SEQ_TILE = 512
FFN_SEQ_TILE = 512
FFN_COL_TILE = 1024
SAMPLE_ROWS = 8
SAMPLE_FF_TILE = 512
VMEM_LIMIT = 56 * 1024 * 1024


def _dot(a, b):
    return jnp.dot(a, b, preferred_element_type=F32)


def _rmsnorm(x, g):
    ms = jnp.mean(x * x, axis=-1, keepdims=True)
    return x * lax.rsqrt(ms + EPS) * g


def _softplus(x):
    return jnp.maximum(x, 0.0) + jnp.log1p(jnp.exp(-jnp.abs(x)))


def _block_diag_dot(xb, w_ref):
    n = D_RNN // V7X_MXU_DIM
    return jnp.concatenate(
        [_dot(xb[:, i * V7X_MXU_DIM:(i + 1) * V7X_MXU_DIM], w_ref[i]) for i in range(n)], axis=1)


def _rglru_coeffs(xc, rwa_ref, rba_ref, rwx_ref, rbx_ref, lam_ref):
    xcb = xc.astype(BF16)
    r = jax.nn.sigmoid(_block_diag_dot(xcb, rwa_ref) + rba_ref[...])
    i = jax.nn.sigmoid(_block_diag_dot(xcb, rwx_ref) + rbx_ref[...])
    log_a = (-RG_C) * r * _softplus(-lam_ref[...])
    a = jnp.exp(log_a)
    y = 1.0 - a * a
    b = jnp.where(y > 0.0, y * lax.rsqrt(y), 0.0) * (i * xc)
    return a, b


def _head_norm_gate(o, gng, g_out):
    parts = []
    for hh in range(GLA_HEADS):
        sl = slice(hh * GLA_DV, (hh + 1) * GLA_DV)
        oh = o[:, sl]
        ms = jnp.mean(oh * oh, axis=-1, keepdims=True)
        parts.append(oh * lax.rsqrt(ms + EPS) * gng[:, sl])
    on = jnp.concatenate(parts, axis=1)
    return on * (g_out * jax.nn.sigmoid(g_out))


def _delayed_rows(prev, u, s):
    ext = jnp.concatenate([prev, u], axis=0)
    return pltpu.roll(ext, s, axis=0)[V7X_SUBLANES:]


def _causal_conv(u, prev, cw, cb, hist):
    y = cb + _delayed_rows(prev, u, hist) * cw[0:1]
    for j in range(1, hist):
        y = y + _delayed_rows(prev, u, hist - j) * cw[j:j + 1]
    return y + u * cw[hist:hist + 1]


def _scan_affine(a, b, h_init, tt):
    width = a.shape[1]
    n_tiles = tt // V7X_SUBLANES
    a = a.reshape(n_tiles, V7X_SUBLANES, width)
    b = b.reshape(n_tiles, V7X_SUBLANES, width)
    sub = lax.broadcasted_iota(jnp.int32, a.shape, 1)
    s = 1
    while s < V7X_SUBLANES:
        keep = sub >= s
        b = jnp.where(keep, a * pltpu.roll(b, s, axis=1) + b, b)
        a = jnp.where(keep, a * pltpu.roll(a, s, axis=1), a)
        s *= 2
    tiles = []
    carry = h_init
    for g in range(n_tiles):
        h = b[g] + a[g] * carry
        carry = h[V7X_SUBLANES - 1:V7X_SUBLANES, :]
        tiles.append(h)
    return jnp.concatenate(tiles, axis=0)


def _gla_decay(glr, wgate_ref, bgate_ref, tt, chunk):
    zg = _dot(glr.astype(BF16), wgate_ref[...]) + bgate_ref[...]
    la = -_softplus(-zg) * (1.0 / GLA_TAU)
    blk = min(tt, V7X_MXU_DIM)
    blocks = [slice(r0, r0 + blk) for r0 in range(0, tt, blk)]
    shift = chunk.bit_length() - 1
    r_i = lax.broadcasted_iota(jnp.int32, (blk, blk), 0)
    c_i = lax.broadcasted_iota(jnp.int32, (blk, blk), 1)
    causal = ((r_i >> shift) == (c_i >> shift)) & (c_i <= r_i)
    cum = jnp.where(causal, 1.0, 0.0).astype(BF16)
    la_hi = la.astype(BF16)
    la_lo = (la - la_hi.astype(F32)).astype(BF16)
    bc = jnp.concatenate([_dot(cum, la_hi[rb]) + _dot(cum, la_lo[rb]) for rb in blocks], axis=0)
    return bc, causal, blocks


def _gla_heads(qk, v, bc, causal, blocks, s_ref, o_ref, chunk):
    blk = blocks[0].stop
    qd_all = (qk[:, 0:QK_DIM] * (GLA_DK ** -0.5) * jnp.exp(bc)).astype(BF16)
    kd_all = (qk[:, QK_DIM:2 * QK_DIM] * jnp.exp(-bc)).astype(BF16)
    for hh in range(GLA_HEADS):
        ks = slice(hh * GLA_DK, (hh + 1) * GLA_DK)
        vs = slice(hh * GLA_DV, (hh + 1) * GLA_DV)
        for rb in blocks:
            att = lax.dot_general(qd_all[rb, ks], kd_all[rb, ks], (((1,), (1,)), ((), ())),
                                  preferred_element_type=F32)
            o_intra = _dot(jnp.where(causal, att, 0.0).astype(BF16), v[rb, vs])
            for c0 in range(0, blk, chunk):
                rs = slice(rb.start + c0, rb.start + c0 + chunk)
                g = bc[rs, ks]
                gl = g[chunk - 1:chunk, :]
                ke = (qk[rs, QK_DIM + hh * GLA_DK:QK_DIM + (hh + 1) * GLA_DK]
                      * jnp.exp(gl - g)).astype(BF16)
                st = s_ref[hh]
                o_ref[rs, vs] = o_intra[c0:c0 + chunk] + _dot(qd_all[rs, ks], st.astype(BF16))
                ds = lax.dot_general(ke, v[rs, vs], (((0,), (0,)), ((), ())),
                                     preferred_element_type=F32)
                dec = jnp.broadcast_to(jnp.exp(gl), (GLA_DK, GLA_DK)).T
                s_ref[hh] = jnp.concatenate([dec] * (GLA_DV // GLA_DK), axis=1) * st + ds


def _mixer_tail(x, zb, ya, o, gng_ref, wbrb_ref, wout_ref):
    on = _head_norm_gate(o, gng_ref[...], zb[:, 0:V_DIM])
    yb = _dot(on.astype(BF16), wbrb_ref[...])
    merged = (jax.nn.sigmoid(zb[:, V_DIM:V_DIM + D_MODEL]) * ya
              + jax.nn.sigmoid(zb[:, V_DIM + D_MODEL:V_DIM + 2 * D_MODEL]) * yb)
    return x + _dot(merged.astype(BF16), wout_ref[...])


def _mixer_body(x, first, rgbuf0_ref, h0_ref, s0_ref, g1_ref, wa_ref, wglr_ref, wb_ref,
                cw_ref, cb_ref, rwa_ref, rba_ref, rwx_ref, rbx_ref, lam_ref,
                wgate_ref, bgate_ref, gng_ref, wbra_ref, wbrb_ref, wout_ref,
                rgbuf_ref, h_ref, s_ref, hist_ref, o_ref, *, tt, chunk):
    hist = RG_CONV - 1

    @pl.when(first)
    def _():
        hist_ref[...] = jnp.zeros(hist_ref.shape, F32)
        hist_ref[V7X_SUBLANES - hist:V7X_SUBLANES, :] = rgbuf0_ref[...]
        h_ref[...] = h0_ref[...]
        s_ref[...] = s0_ref[...]

    xn = _rmsnorm(x, g1_ref[...]).astype(BF16)

    rg_x = _dot(xn, wa_ref[:, 0:D_RNN])
    glr = _dot(xn, wglr_ref[...])
    xc = _causal_conv(rg_x, hist_ref[...], cw_ref[...], cb_ref[...], hist)
    hist_ref[...] = rg_x[tt - V7X_SUBLANES:tt, :]
    rgbuf_ref[...] = hist_ref[V7X_SUBLANES - hist:V7X_SUBLANES, :]
    qk = _dot(xn, wa_ref[:, Q_OFF:V_OFF])
    a, b = _rglru_coeffs(xc, rwa_ref, rba_ref, rwx_ref, rbx_ref, lam_ref)
    rg_y = _dot(xn, wa_ref[:, D_RNN:2 * D_RNN])
    v = _dot(xn, wa_ref[:, V_OFF:V_OFF + V_DIM]).astype(BF16)
    h = _scan_affine(a, b, h_ref[...], tt)
    h_ref[...] = h[tt - 1:tt, :]
    zb = _dot(xn, wb_ref[...])
    ya = _dot((h * jax.nn.gelu(rg_y)).astype(BF16), wbra_ref[...])

    bc, causal, blocks = _gla_decay(glr, wgate_ref, bgate_ref, tt, chunk)
    _gla_heads(qk, v, bc, causal, blocks, s_ref, o_ref, chunk)
    return _mixer_tail(x, zb, ya, o_ref[...], gng_ref, wbrb_ref, wout_ref)


def _ffn_body(x, first, buf0_ref, g2_ref, wup_ref, cw_ref, cb_ref, wdn_ref, gf_ref,
              buf_ref, hist_ref, *, tt, cf, final):
    hist = FFN_CONV - 1

    @pl.when(first)
    def _():
        hist_ref[...] = jnp.zeros(hist_ref.shape, F32)
        hist_ref[V7X_SUBLANES - hist:V7X_SUBLANES, :] = buf0_ref[...]

    xn = _rmsnorm(x, g2_ref[...]).astype(BF16)

    def up_proj(j):
        return [(slice(off, off + cf), _dot(xn, wup_ref[:, off:off + cf]))
                for off in (j * cf, D_FF + j * cf)]

    acc = x
    n_chunks = D_FF // cf
    ups = up_proj(0)
    for j in range(n_chunks):
        nxt = up_proj(j + 1) if j + 1 < n_chunks else None
        halves = []
        for cs, u in ups:
            halves.append(_causal_conv(u, hist_ref[:, cs], cw_ref[:, cs], cb_ref[:, cs], hist))
            hist_ref[:, cs] = u[tt - V7X_SUBLANES:tt, :]
        hmid = (jax.nn.gelu(halves[0]) * halves[1]).astype(BF16)
        acc = acc + _dot(hmid, wdn_ref[j * cf:(j + 1) * cf, :])
        ups = nxt
    buf_ref[...] = hist_ref[V7X_SUBLANES - hist:V7X_SUBLANES, :]
    return _rmsnorm(acc, gf_ref[...]) if final else acc


N_MIXER_IN = 20
N_FFN_IN = 7


def _mixer_seq_kernel(x_ref, *refs, tt, chunk):
    ins, (xo_ref, rgbuf_ref, h_ref, s_ref, hist_ref, o_ref) = refs[:N_MIXER_IN], refs[N_MIXER_IN:]
    xo_ref[...] = _mixer_body(x_ref[...], pl.program_id(1) == 0, *ins,
                              rgbuf_ref, h_ref, s_ref, hist_ref, o_ref, tt=tt, chunk=chunk)


def _ffn_seq_kernel(x_ref, *refs, tt, cf, final):
    ins, (xo_ref, buf_ref, hist_ref) = refs[:N_FFN_IN], refs[N_FFN_IN:]
    xo_ref[...] = _ffn_body(x_ref[...], pl.program_id(1) == 0, *ins, buf_ref, hist_ref,
                            tt=tt, cf=cf, final=final)


def _layer_seq_kernel(x_ref, *refs, tt, chunk, cf, final):
    m_in, f_in = refs[:N_MIXER_IN], refs[N_MIXER_IN:N_MIXER_IN + N_FFN_IN]
    (xo_ref, rgbuf_ref, h_ref, s_ref, buf_ref,
     hist_ref, o_ref, fhist_ref) = refs[N_MIXER_IN + N_FFN_IN:]
    first = pl.program_id(1) == 0
    x1 = _mixer_body(x_ref[...], first, *m_in, rgbuf_ref, h_ref, s_ref, hist_ref, o_ref,
                     tt=tt, chunk=chunk)
    xo_ref[...] = _ffn_body(x1, first, *f_in, buf_ref, fhist_ref, tt=tt, cf=cf, final=final)


def _layer_spec(tail, l, single=True):
    zeros = (0,) * len(tail)
    kw = dict(pipeline_mode=pl.Buffered(1)) if single else {}
    return pl.BlockSpec((None,) + tuple(tail), lambda *_: (l,) + zeros, **kw)


def _mixer_seq(x, rgbuf0, h0, s0, w, l, tt, chunk):
    bsz, t_len, _ = x.shape
    bi = 1 if rgbuf0.shape[0] == bsz else 0

    def state_spec(tail):
        zeros = (0,) * len(tail)
        return pl.BlockSpec((None,) + tuple(tail), lambda b, t: (b * bi,) + zeros)

    def out_state_spec(tail):
        zeros = (0,) * len(tail)
        return pl.BlockSpec((None,) + tuple(tail), lambda b, t: (b,) + zeros)

    x_spec = pl.BlockSpec((None, tt, D_MODEL), lambda b, t: (b, t, 0))
    hist = RG_CONV - 1
    names = ("norm1_g", "w_a", "w_glr", "w_b", "rg_conv_w", "rg_conv_b", "rg_wa", "rg_ba",
             "rg_wx", "rg_bx", "rg_lambda", "gla_w_gate", "gla_b_gate", "gla_norm_g",
             "w_branch_a", "w_branch_b", "w_out")
    return pl.pallas_call(
        functools.partial(_mixer_seq_kernel, tt=tt, chunk=chunk),
        grid=(bsz, t_len // tt),
        in_specs=[x_spec, state_spec((hist, D_RNN)), state_spec((1, D_RNN)),
                  state_spec((GLA_HEADS, GLA_DK, GLA_DV))]
                 + [_layer_spec(w[n].shape[1:], l) for n in names],
        out_specs=[x_spec, out_state_spec((hist, D_RNN)), out_state_spec((1, D_RNN)),
                   out_state_spec((GLA_HEADS, GLA_DK, GLA_DV))],
        out_shape=[jax.ShapeDtypeStruct(x.shape, F32),
                   jax.ShapeDtypeStruct((bsz, hist, D_RNN), F32),
                   jax.ShapeDtypeStruct((bsz, 1, D_RNN), F32),
                   jax.ShapeDtypeStruct((bsz, GLA_HEADS, GLA_DK, GLA_DV), F32)],
        scratch_shapes=[pltpu.VMEM((V7X_SUBLANES, D_RNN), F32),
                        pltpu.VMEM((tt, V_DIM), F32)],
        compiler_params=pltpu.CompilerParams(
            dimension_semantics=("arbitrary", "arbitrary"), vmem_limit_bytes=VMEM_LIMIT),
        name="mixer_seq",
    )(x, rgbuf0, h0, s0, *[w[n] for n in names])


def _ffn_seq(x, buf0, w, l, tt, final):
    bsz, t_len, _ = x.shape
    bi = 1 if buf0.shape[0] == bsz else 0
    hist = FFN_CONV - 1
    cf = FFN_COL_TILE
    x_spec = pl.BlockSpec((None, tt, D_MODEL), lambda b, t: (b, t, 0))
    names = ("norm2_g", "w_ffn_up", "ffn_conv_w", "ffn_conv_b", "w_ffn_down")
    return pl.pallas_call(
        functools.partial(_ffn_seq_kernel, tt=tt, cf=cf, final=final),
        grid=(bsz, t_len // tt),
        in_specs=[x_spec, pl.BlockSpec((None, hist, 2 * D_FF), lambda b, t: (b * bi, 0, 0))]
                 + [_layer_spec(w[n].shape[1:], l) for n in names]
                 + [pl.BlockSpec((1, D_MODEL), lambda b, t: (0, 0))],
        out_specs=[x_spec, pl.BlockSpec((None, hist, 2 * D_FF), lambda b, t: (b, 0, 0))],
        out_shape=[jax.ShapeDtypeStruct(x.shape, F32),
                   jax.ShapeDtypeStruct((bsz, hist, 2 * D_FF), F32)],
        scratch_shapes=[pltpu.VMEM((V7X_SUBLANES, 2 * D_FF), F32)],
        compiler_params=pltpu.CompilerParams(
            dimension_semantics=("arbitrary", "arbitrary"), vmem_limit_bytes=VMEM_LIMIT),
        name="ffn_seq",
    )(x, buf0, *[w[n] for n in names], w["final_norm_g"])


MIXER_WEIGHTS = ("norm1_g", "w_a", "w_glr", "w_b", "rg_conv_w", "rg_conv_b", "rg_wa", "rg_ba",
                 "rg_wx", "rg_bx", "rg_lambda", "gla_w_gate", "gla_b_gate", "gla_norm_g",
                 "w_branch_a", "w_branch_b", "w_out")
FFN_WEIGHTS = ("norm2_g", "w_ffn_up", "ffn_conv_w", "ffn_conv_b", "w_ffn_down")


def _layer_seq(x, rgbuf0, h0, s0, fbuf0, w, l, tt, chunk, final):
    bsz, t_len, _ = x.shape
    bi = 1 if rgbuf0.shape[0] == bsz else 0
    hist, fhist = RG_CONV - 1, FFN_CONV - 1

    def state_spec(tail, per_row):
        zeros = (0,) * len(tail)
        return pl.BlockSpec((None,) + tuple(tail), lambda b, t: (b * per_row,) + zeros)

    x_spec = pl.BlockSpec((None, tt, D_MODEL), lambda b, t: (b, t, 0))
    state_tails = [(hist, D_RNN), (1, D_RNN), (GLA_HEADS, GLA_DK, GLA_DV)]
    ffn_tail = (fhist, 2 * D_FF)
    return pl.pallas_call(
        functools.partial(_layer_seq_kernel, tt=tt, chunk=chunk, cf=FFN_COL_TILE, final=final),
        grid=(bsz, t_len // tt),
        in_specs=[x_spec] + [state_spec(t, bi) for t in state_tails]
                 + [_layer_spec(w[n].shape[1:], l) for n in MIXER_WEIGHTS]
                 + [state_spec(ffn_tail, bi)]
                 + [_layer_spec(w[n].shape[1:], l) for n in FFN_WEIGHTS]
                 + [pl.BlockSpec((1, D_MODEL), lambda b, t: (0, 0))],
        out_specs=[x_spec] + [state_spec(t, 1) for t in state_tails] + [state_spec(ffn_tail, 1)],
        out_shape=[jax.ShapeDtypeStruct(x.shape, F32)]
                  + [jax.ShapeDtypeStruct((bsz,) + t, F32) for t in state_tails]
                  + [jax.ShapeDtypeStruct((bsz,) + ffn_tail, F32)],
        scratch_shapes=[pltpu.VMEM((V7X_SUBLANES, D_RNN), F32),
                        pltpu.VMEM((tt, V_DIM), F32),
                        pltpu.VMEM((V7X_SUBLANES, 2 * D_FF), F32)],
        compiler_params=pltpu.CompilerParams(
            dimension_semantics=("arbitrary", "arbitrary"), vmem_limit_bytes=VMEM_LIMIT),
        name="layer_seq",
    )(x, rgbuf0, h0, s0, *[w[n] for n in MIXER_WEIGHTS], fbuf0,
      *[w[n] for n in FFN_WEIGHTS], w["final_norm_g"])


def _trunk_seq(x, states, w, tt, ffn_tt, chunk):
    outs = []
    for l in range(DEPTH):
        rgbuf0, h0, s0, fbuf0 = states[l]
        final = l == DEPTH - 1
        if x.shape[1] == tt:
            x, rgbuf, h, s, fbuf = _layer_seq(x, rgbuf0, h0, s0, fbuf0, w, l, tt, chunk, final)
        else:
            x, rgbuf, h, s = _mixer_seq(x, rgbuf0, h0, s0, w, l, tt, chunk)
            x, fbuf = _ffn_seq(x, fbuf0, w, l, ffn_tt, final=final)
        outs.append((rgbuf, h, s, fbuf))
    return x, outs


def _sample_pre_kernel(x_ref, rb0_ref, rb1_ref, rb2_ref, h0_ref, g1_ref, wa_ref, wglr_ref, wb_ref,
                       cw_ref, cb_ref, rwa_ref, rba_ref, rwx_ref, rbx_ref, lam_ref,
                       wgate_ref, bgate_ref, wbra_ref,
                       rgx_ref, h_ref, ya_ref, at_ref, kt_ref, wt_ref, v_ref, okv_ref, zb_ref):
    hist = RG_CONV - 1
    rb_refs = (rb0_ref, rb1_ref, rb2_ref)
    xn = _rmsnorm(x_ref[...], g1_ref[...]).astype(BF16)
    za = _dot(xn, wa_ref[...])
    glr = _dot(xn, wglr_ref[...])
    zb_ref[...] = _dot(xn, wb_ref[...])

    rg_x = za[:, 0:D_RNN]
    rgx_ref[...] = rg_x
    cw = cw_ref[...]
    xc = cb_ref[...] + rb_refs[0][...] * cw[0:1]
    for j in range(1, hist):
        xc = xc + rb_refs[j][...] * cw[j:j + 1]
    xc = xc + rg_x * cw[hist:hist + 1]
    a, b = _rglru_coeffs(xc, rwa_ref, rba_ref, rwx_ref, rbx_ref, lam_ref)
    h = a * h0_ref[...] + b
    h_ref[...] = h
    ya_ref[...] = _dot((h * jax.nn.gelu(za[:, D_RNN:2 * D_RNN])).astype(BF16), wbra_ref[...])

    zg = _dot(glr.astype(BF16), wgate_ref[...]) + bgate_ref[...]
    alpha = jnp.exp(-_softplus(-zg) * (1.0 / GLA_TAU))
    q = za[:, Q_OFF:Q_OFF + QK_DIM] * (GLA_DK ** -0.5)
    k = za[:, K_OFF:K_OFF + QK_DIM]
    v = za[:, V_OFF:V_OFF + V_DIM]
    at_ref[...] = alpha.T
    kt_ref[...] = k.T
    wt_ref[...] = (q * alpha).T
    v_ref[...] = v
    qk = q * k
    okv_ref[...] = jnp.concatenate(
        [jnp.sum(qk[:, hh * GLA_DK:(hh + 1) * GLA_DK], axis=-1, keepdims=True)
         * v[:, hh * GLA_DV:(hh + 1) * GLA_DV] for hh in range(GLA_HEADS)], axis=1)


def _row_onehot(nb, row):
    rows = lax.broadcasted_iota(jnp.int32, (nb, GLA_DV), 0)
    return jnp.where(rows == row, 1.0, 0.0).astype(BF16)


def _sample_read_kernel(s0_ref, wt_ref, okv_ref, o_ref, *, nb):
    i = pl.program_id(0)
    wb = wt_ref[...].astype(BF16)
    for j in range(SAMPLE_ROWS):
        cw = _dot(wb, _row_onehot(nb, i * SAMPLE_ROWS + j))
        for hh in range(GLA_HEADS):
            ks = slice(hh * GLA_DK, (hh + 1) * GLA_DK)
            vs = slice(hh * GLA_DV, (hh + 1) * GLA_DV)
            o_ref[j:j + 1, vs] = (jnp.sum(cw[ks] * s0_ref[j, hh], axis=0, keepdims=True)
                                  + okv_ref[j:j + 1, vs])


def _sample_update_kernel(s0_ref, at_ref, kt_ref, v_ref, s_ref, *, nb):
    i = pl.program_id(1)
    al = at_ref[...]
    a_hi = al.astype(BF16)
    r1 = al - a_hi.astype(F32)
    a_mid = r1.astype(BF16)
    a_lo = (r1 - a_mid.astype(F32)).astype(BF16)
    kb = kt_ref[...].astype(BF16)
    for j in range(SAMPLE_ROWS):
        e = _row_onehot(nb, i * SAMPLE_ROWS + j)
        ca = _dot(a_hi, e) + _dot(a_mid, e) + _dot(a_lo, e)
        ck = _dot(kb, e)
        for hh in range(GLA_HEADS):
            ks = slice(hh * GLA_DK, (hh + 1) * GLA_DK)
            vs = slice(hh * GLA_DV, (hh + 1) * GLA_DV)
            s_ref[j, hh] = ca[ks] * s0_ref[j, hh] + ck[ks] * v_ref[j:j + 1, vs]


def _sample_post_kernel(x_ref, ya_ref, o_ref, zb_ref, gng_ref, wbrb_ref, wout_ref, g2_ref,
                        wupg_ref, wupv_ref, cwg_ref, cwv_ref, cbg_ref, cbv_ref,
                        bg0_ref, bv0_ref, bg1_ref, bv1_ref, wdn_ref, gf_ref,
                        xo_ref, ug_ref, uv_ref, xn_ref, *, final):
    j = pl.program_id(0)

    @pl.when(j == 0)
    def _():
        zb = zb_ref[...]
        on = _head_norm_gate(o_ref[...], gng_ref[...], zb[:, 0:V_DIM])
        yb = _dot(on.astype(BF16), wbrb_ref[...])
        merged = (jax.nn.sigmoid(zb[:, V_DIM:V_DIM + D_MODEL]) * ya_ref[...]
                  + jax.nn.sigmoid(zb[:, V_DIM + D_MODEL:V_DIM + 2 * D_MODEL]) * yb)
        x1 = x_ref[...] + _dot(merged.astype(BF16), wout_ref[...])
        xo_ref[...] = x1
        xn_ref[...] = _rmsnorm(x1, g2_ref[...]).astype(BF16)

    xn = xn_ref[...]
    ug = _dot(xn, wupg_ref[...])
    uv = _dot(xn, wupv_ref[...])
    ug_ref[...] = ug
    uv_ref[...] = uv
    cwg = cwg_ref[...]
    cwv = cwv_ref[...]
    gate = cbg_ref[...] + bg0_ref[...] * cwg[0:1] + bg1_ref[...] * cwg[1:2] + ug * cwg[2:3]
    val = cbv_ref[...] + bv0_ref[...] * cwv[0:1] + bv1_ref[...] * cwv[1:2] + uv * cwv[2:3]
    xo_ref[...] += _dot((jax.nn.gelu(gate) * val).astype(BF16), wdn_ref[...])

    if final:
        @pl.when(j == pl.num_programs(0) - 1)
        def _():
            xo_ref[...] = _rmsnorm(xo_ref[...], gf_ref[...])


def _full_spec(shape):
    zeros = (0,) * len(shape)
    return pl.BlockSpec(tuple(shape), lambda *_: zeros)


def _sample_layer(x, rg_rows, h_all, s_all, ffn_rows, w, l, final):
    nb = x.shape[0]
    params = pltpu.CompilerParams(dimension_semantics=("arbitrary",),
                                  vmem_limit_bytes=VMEM_LIMIT)

    pre_names = ("norm1_g", "w_a", "w_glr", "w_b", "rg_conv_w", "rg_conv_b", "rg_wa", "rg_ba",
                 "rg_wx", "rg_bx", "rg_lambda", "gla_w_gate", "gla_b_gate", "w_branch_a")
    pre_out = [(nb, D_RNN), (nb, D_RNN), (nb, D_MODEL), (QK_DIM, nb), (QK_DIM, nb), (QK_DIM, nb),
               (nb, V_DIM), (nb, V_DIM), (nb, W_B_COLS)]
    rg_x, h, ya, a_t, k_t, w_t, v, okv, zb = pl.pallas_call(
        _sample_pre_kernel,
        grid=(1,),
        in_specs=[_full_spec(x.shape)]
                 + [_layer_spec(r.shape[1:], l, single=False) for r in rg_rows]
                 + [_layer_spec(h_all.shape[1:], l, single=False)]
                 + [_layer_spec(w[n].shape[1:], l) for n in pre_names],
        out_specs=[_full_spec(s) for s in pre_out],
        out_shape=[jax.ShapeDtypeStruct(s, F32) for s in pre_out],
        compiler_params=params,
        name="sample_pre",
    )(x, *rg_rows, h_all, *[w[n] for n in pre_names])

    s_blk = (None, SAMPLE_ROWS, GLA_HEADS, GLA_DK, GLA_DV)
    row_spec = pl.BlockSpec((SAMPLE_ROWS, V_DIM), lambda i: (i, 0))
    o = pl.pallas_call(
        functools.partial(_sample_read_kernel, nb=nb),
        grid=(nb // SAMPLE_ROWS,),
        in_specs=[pl.BlockSpec(s_blk, lambda i: (l, i, 0, 0, 0)), _full_spec((QK_DIM, nb)),
                  row_spec],
        out_specs=row_spec,
        out_shape=jax.ShapeDtypeStruct((nb, V_DIM), F32),
        compiler_params=params,
        name="sample_read",
    )(s_all, w_t, okv)

    tf = SAMPLE_FF_TILE
    nf = D_FF // tf

    def lcol(rows, off):
        return pl.BlockSpec((None, rows, tf), lambda j: (l, 0, off + j))

    ucol = pl.BlockSpec((nb, tf), lambda j: (0, j))
    xo, ug, uv = pl.pallas_call(
        functools.partial(_sample_post_kernel, final=final),
        grid=(nf,),
        in_specs=[_full_spec(x.shape), _full_spec(ya.shape), _full_spec(o.shape),
                  _full_spec(zb.shape),
                  _layer_spec(w["gla_norm_g"].shape[1:], l, single=False),
                  _layer_spec(w["w_branch_b"].shape[1:], l, single=False),
                  _layer_spec(w["w_out"].shape[1:], l, single=False),
                  _layer_spec(w["norm2_g"].shape[1:], l, single=False),
                  lcol(D_MODEL, 0), lcol(D_MODEL, nf),
                  lcol(FFN_CONV, 0), lcol(FFN_CONV, nf),
                  lcol(1, 0), lcol(1, nf),
                  lcol(nb, 0), lcol(nb, nf), lcol(nb, 0), lcol(nb, nf),
                  pl.BlockSpec((None, tf, D_MODEL), lambda j: (l, j, 0)),
                  _full_spec((1, D_MODEL))],
        out_specs=[_full_spec(x.shape), ucol, ucol],
        out_shape=[jax.ShapeDtypeStruct(x.shape, F32),
                   jax.ShapeDtypeStruct((nb, D_FF), F32), jax.ShapeDtypeStruct((nb, D_FF), F32)],
        scratch_shapes=[pltpu.VMEM((nb, D_MODEL), BF16)],
        compiler_params=params,
        name="sample_post",
    )(x, ya, o, zb, w["gla_norm_g"], w["w_branch_b"], w["w_out"], w["norm2_g"],
      w["w_ffn_up"], w["w_ffn_up"], w["ffn_conv_w"], w["ffn_conv_w"],
      w["ffn_conv_b"], w["ffn_conv_b"], ffn_rows[0], ffn_rows[0], ffn_rows[1], ffn_rows[1],
      w["w_ffn_down"], w["final_norm_g"])
    return xo, rg_x, h, (a_t, k_t, v), (ug, uv)


def _sample_state_update(s_all, a_t, k_t, v):
    nb = v.shape[1]
    s_blk = (None, SAMPLE_ROWS, GLA_HEADS, GLA_DK, GLA_DV)
    s_spec = pl.BlockSpec(s_blk, lambda l, i: (l, i, 0, 0, 0))
    t_spec = pl.BlockSpec((None, QK_DIM, nb), lambda l, i: (l, 0, 0))
    return pl.pallas_call(
        functools.partial(_sample_update_kernel, nb=nb),
        grid=(DEPTH, nb // SAMPLE_ROWS),
        in_specs=[s_spec, t_spec, t_spec,
                  pl.BlockSpec((None, SAMPLE_ROWS, V_DIM), lambda l, i: (l, i, 0))],
        out_specs=s_spec,
        out_shape=jax.ShapeDtypeStruct(s_all.shape, F32),
        compiler_params=pltpu.CompilerParams(
            dimension_semantics=("arbitrary", "arbitrary"), vmem_limit_bytes=VMEM_LIMIT),
        name="sample_update",
    )(s_all, a_t, k_t, v)


def _prep_weights(norm1_g, w_in, rg_conv_w, rg_conv_b, rg_wa, rg_ba, rg_wx, rg_bx, rg_lambda,
                  gla_w_gate, gla_b_gate, gla_norm_g, w_branch_a, w_branch_b, w_out, norm2_g,
                  w_ffn_up, ffn_conv_w, ffn_conv_b, w_ffn_down, final_norm_g):
    def vec(a):
        return a.reshape(DEPTH, 1, -1).astype(F32)

    def block_diag(wb):
        per = V7X_MXU_DIM // RG_BW
        w5 = wb.reshape(DEPTH, RG_BLOCKS // per, per, RG_BW, RG_BW)
        bd = jnp.einsum("lgicd,ij->lgicjd", w5, jnp.eye(per, dtype=wb.dtype))
        return bd.reshape(DEPTH, RG_BLOCKS // per, V7X_MXU_DIM, V7X_MXU_DIM).astype(BF16)

    pad_r = V7X_LANES - GLA_RANK
    return dict(
        norm1_g=vec(norm1_g),
        w_a=w_in[:, :, 0:W_A_COLS].astype(BF16),
        w_glr=jnp.pad(w_in[:, :, W_GLR_OFF:W_B_OFF], ((0, 0), (0, 0), (0, pad_r))).astype(BF16),
        w_b=w_in[:, :, W_B_OFF:W_B_OFF + W_B_COLS].astype(BF16),
        rg_conv_w=rg_conv_w.astype(F32), rg_conv_b=vec(rg_conv_b),
        rg_wa=block_diag(rg_wa), rg_ba=vec(rg_ba), rg_wx=block_diag(rg_wx), rg_bx=vec(rg_bx),
        rg_lambda=vec(rg_lambda),
        gla_w_gate=jnp.pad(gla_w_gate, ((0, 0), (0, pad_r), (0, 0))).astype(BF16),
        gla_b_gate=vec(gla_b_gate), gla_norm_g=vec(gla_norm_g),
        w_branch_a=w_branch_a.astype(BF16), w_branch_b=w_branch_b.astype(BF16),
        w_out=w_out.astype(BF16), norm2_g=vec(norm2_g),
        w_ffn_up=w_ffn_up.astype(BF16), ffn_conv_w=ffn_conv_w.astype(F32),
        ffn_conv_b=vec(ffn_conv_b), w_ffn_down=w_ffn_down.astype(BF16),
        final_norm_g=final_norm_g.reshape(1, D_MODEL).astype(F32),
    )


def kernel(x_prompt, x_sample, state_rg_conv, state_rg_h, state_gla, state_ffn_conv, meta_tokens, norm1_g, w_in, rg_conv_w, rg_conv_b, rg_wa, rg_ba, rg_wx, rg_bx, rg_lambda, gla_w_gate, gla_b_gate, gla_norm_g, w_branch_a, w_branch_b, w_out, norm2_g, w_ffn_up, ffn_conv_w, ffn_conv_b, w_ffn_down, final_norm_g):
    w = _prep_weights(norm1_g, w_in, rg_conv_w, rg_conv_b, rg_wa, rg_ba, rg_wx, rg_bx, rg_lambda,
                      gla_w_gate, gla_b_gate, gla_norm_g, w_branch_a, w_branch_b, w_out, norm2_g,
                      w_ffn_up, ffn_conv_w, ffn_conv_b, w_ffn_down, final_norm_g)
    bsz = x_prompt.shape[0]

    zero_states = [(jnp.zeros((1, RG_CONV - 1, D_RNN), F32), jnp.zeros((1, 1, D_RNN), F32),
                    jnp.zeros((1, GLA_HEADS, GLA_DK, GLA_DV), F32),
                    jnp.zeros((1, FFN_CONV - 1, 2 * D_FF), F32))] * DEPTH
    _, meta_states = _trunk_seq(meta_tokens.astype(F32)[None], zero_states, w,
                                tt=N_META, ffn_tt=N_META, chunk=N_META)

    y_prompt, p_states = _trunk_seq(x_prompt, meta_states, w, tt=SEQ_TILE, ffn_tt=FFN_SEQ_TILE,
                                    chunk=GLA_CHUNK)
    rg_conv_prompt = jnp.stack([s[0] for s in p_states])
    rg_h_prompt = jnp.stack([s[1].reshape(bsz, D_RNN) for s in p_states])
    gla_prompt = jnp.stack([s[2] for s in p_states])
    ffn_conv_prompt = jnp.stack([s[3] for s in p_states])

    nb = x_sample.shape[0]
    xs = x_sample.reshape(nb, D_MODEL)
    rg_rows = [state_rg_conv[:, :, j] for j in range(RG_CONV - 1)]
    ffn_rows = [state_ffn_conv[:, :, j] for j in range(FFN_CONV - 1)]
    rg_xs, hs, akv, us = [], [], [], []
    for l in range(DEPTH):
        xs, rg_x, h, akv_l, u_l = _sample_layer(xs, rg_rows, state_rg_h, state_gla, ffn_rows,
                                                w, l, final=(l == DEPTH - 1))
        rg_xs.append(rg_x)
        hs.append(h)
        akv.append(akv_l)
        us.append(jnp.concatenate(u_l, axis=1))
    gla_sample = _sample_state_update(state_gla, *[jnp.stack(t) for t in zip(*akv)])
    rg_conv_sample = jnp.stack(rg_rows[1:] + [jnp.stack(rg_xs)], axis=2)
    ffn_conv_sample = jnp.stack(ffn_rows[1:] + [jnp.stack(us)], axis=2)
    y_sample = xs.reshape(x_sample.shape)
    return (y_prompt, y_sample, rg_conv_prompt, rg_h_prompt, gla_prompt, ffn_conv_prompt,
            rg_conv_sample, jnp.stack(hs), gla_sample, ffn_conv_sample)
```

```python
import functools

import jax
import jax.numpy as jnp
from jax import lax
from jax.experimental import pallas as pl
from jax.experimental.pallas import tpu as pltpu

F32 = jnp.float32
BF16 = jnp.bfloat16

D_MODEL = 1024
DEPTH = 4
D_RNN = D_MODEL
RG_BLOCKS = 16
RG_BW = D_RNN // RG_BLOCKS
RG_CONV = 4
RG_C = 8.0
GLA_HEADS = 4
GLA_DK = 128
GLA_DV = 256
QK_DIM = GLA_HEADS * GLA_DK
V_DIM = GLA_HEADS * GLA_DV
GLA_RANK = 16
GLA_TAU = 16.0
GLA_CHUNK = 64
D_FF = 3 * D_MODEL
FFN_CONV = 3
N_META = 16
EPS = 1e-6

V7X_MXU_DIM = 256
V7X_LANES = 128
V7X_SUBLANES = 8

W_A_COLS = 2 * D_RNN + 2 * QK_DIM + V_DIM
W_GLR_OFF = W_A_COLS
W_B_OFF = W_A_COLS + GLA_RANK
W_B_COLS = V_DIM + 2 * D_MODEL
Q_OFF, K_OFF, V_OFF = 2 * D_RNN, 2 * D_RNN + QK_DIM, 2 * D_RNN + 2 * QK_DIM

SEQ_TILE = 512
FFN_SEQ_TILE = 512
FFN_COL_TILE = 1024
SAMPLE_ROWS = 8
SAMPLE_FF_TILE = 512
VMEM_LIMIT = 56 * 1024 * 1024


def _dot(a, b):
    return jnp.dot(a, b, preferred_element_type=F32)


def _rmsnorm(x, g):
    ms = jnp.mean(x * x, axis=-1, keepdims=True)
    return x * lax.rsqrt(ms + EPS) * g


def _softplus(x):
    return jnp.maximum(x, 0.0) + jnp.log1p(jnp.exp(-jnp.abs(x)))


def _block_diag_dot(xb, w_ref):
    n = D_RNN // V7X_MXU_DIM
    return jnp.concatenate(
        [_dot(xb[:, i * V7X_MXU_DIM:(i + 1) * V7X_MXU_DIM], w_ref[i]) for i in range(n)], axis=1)


def _rglru_coeffs(xc, rwa_ref, rba_ref, rwx_ref, rbx_ref, lam_ref):
    xcb = xc.astype(BF16)
    r = jax.nn.sigmoid(_block_diag_dot(xcb, rwa_ref) + rba_ref[...])
    i = jax.nn.sigmoid(_block_diag_dot(xcb, rwx_ref) + rbx_ref[...])
    log_a = (-RG_C) * r * _softplus(-lam_ref[...])
    a = jnp.exp(log_a)
    y = 1.0 - a * a
    b = jnp.where(y > 0.0, y * lax.rsqrt(y), 0.0) * (i * xc)
    return a, b


def _head_norm_gate(o, gng, g_out):
    parts = []
    for hh in range(GLA_HEADS):
        sl = slice(hh * GLA_DV, (hh + 1) * GLA_DV)
        oh = o[:, sl]
        ms = jnp.mean(oh * oh, axis=-1, keepdims=True)
        parts.append(oh * lax.rsqrt(ms + EPS) * gng[:, sl])
    on = jnp.concatenate(parts, axis=1)
    return on * (g_out * jax.nn.sigmoid(g_out))


def _delayed_rows(prev, u, s):
    ext = jnp.concatenate([prev, u], axis=0)
    return pltpu.roll(ext, s, axis=0)[V7X_SUBLANES:]


def _causal_conv(u, prev, cw, cb, hist):
    y = cb + _delayed_rows(prev, u, hist) * cw[0:1]
    for j in range(1, hist):
        y = y + _delayed_rows(prev, u, hist - j) * cw[j:j + 1]
    return y + u * cw[hist:hist + 1]


def _scan_affine(a, b, h_init, tt):
    width = a.shape[1]
    n_tiles = tt // V7X_SUBLANES
    a = a.reshape(n_tiles, V7X_SUBLANES, width)
    b = b.reshape(n_tiles, V7X_SUBLANES, width)
    sub = lax.broadcasted_iota(jnp.int32, a.shape, 1)
    s = 1
    while s < V7X_SUBLANES:
        keep = sub >= s
        b = jnp.where(keep, a * pltpu.roll(b, s, axis=1) + b, b)
        a = jnp.where(keep, a * pltpu.roll(a, s, axis=1), a)
        s *= 2
    tiles = []
    carry = h_init
    for g in range(n_tiles):
        h = b[g] + a[g] * carry
        carry = h[V7X_SUBLANES - 1:V7X_SUBLANES, :]
        tiles.append(h)
    return jnp.concatenate(tiles, axis=0)


def _gla_decay(glr, wgate_ref, bgate_ref, tt, chunk):
    zg = _dot(glr.astype(BF16), wgate_ref[...]) + bgate_ref[...]
    la = -_softplus(-zg) * (1.0 / GLA_TAU)
    blk = min(tt, V7X_MXU_DIM)
    blocks = [slice(r0, r0 + blk) for r0 in range(0, tt, blk)]
    shift = chunk.bit_length() - 1
    r_i = lax.broadcasted_iota(jnp.int32, (blk, blk), 0)
    c_i = lax.broadcasted_iota(jnp.int32, (blk, blk), 1)
    causal = ((r_i >> shift) == (c_i >> shift)) & (c_i <= r_i)
    cum = jnp.where(causal, 1.0, 0.0).astype(BF16)
    la_hi = la.astype(BF16)
    la_lo = (la - la_hi.astype(F32)).astype(BF16)
    bc = jnp.concatenate([_dot(cum, la_hi[rb]) + _dot(cum, la_lo[rb]) for rb in blocks], axis=0)
    return bc, causal, blocks


def _gla_heads(qk, v, bc, causal, blocks, s_ref, o_ref, chunk):
    blk = blocks[0].stop
    qd_all = (qk[:, 0:QK_DIM] * (GLA_DK ** -0.5) * jnp.exp(bc)).astype(BF16)
    kd_all = (qk[:, QK_DIM:2 * QK_DIM] * jnp.exp(-bc)).astype(BF16)
    for hh in range(GLA_HEADS):
        ks = slice(hh * GLA_DK, (hh + 1) * GLA_DK)
        vs = slice(hh * GLA_DV, (hh + 1) * GLA_DV)
        for rb in blocks:
            att = lax.dot_general(qd_all[rb, ks], kd_all[rb, ks], (((1,), (1,)), ((), ())),
                                  preferred_element_type=F32)
            o_intra = _dot(jnp.where(causal, att, 0.0).astype(BF16), v[rb, vs])
            for c0 in range(0, blk, chunk):
                rs = slice(rb.start + c0, rb.start + c0 + chunk)
                g = bc[rs, ks]
                gl = g[chunk - 1:chunk, :]
                ke = (qk[rs, QK_DIM + hh * GLA_DK:QK_DIM + (hh + 1) * GLA_DK]
                      * jnp.exp(gl - g)).astype(BF16)
                st = s_ref[hh]
                o_ref[rs, vs] = o_intra[c0:c0 + chunk] + _dot(qd_all[rs, ks], st.astype(BF16))
                ds = lax.dot_general(ke, v[rs, vs], (((0,), (0,)), ((), ())),
                                     preferred_element_type=F32)
                dec = jnp.broadcast_to(jnp.exp(gl), (GLA_DK, GLA_DK)).T
                s_ref[hh] = jnp.concatenate([dec] * (GLA_DV // GLA_DK), axis=1) * st + ds


def _mixer_tail(x, zb, ya, o, gng_ref, wbrb_ref, wout_ref):
    on = _head_norm_gate(o, gng_ref[...], zb[:, 0:V_DIM])
    yb = _dot(on.astype(BF16), wbrb_ref[...])
    merged = (jax.nn.sigmoid(zb[:, V_DIM:V_DIM + D_MODEL]) * ya
              + jax.nn.sigmoid(zb[:, V_DIM + D_MODEL:V_DIM + 2 * D_MODEL]) * yb)
    return x + _dot(merged.astype(BF16), wout_ref[...])


def _mixer_body(x, first, rgbuf0_ref, h0_ref, s0_ref, g1_ref, wa_ref, wglr_ref, wb_ref,
                cw_ref, cb_ref, rwa_ref, rba_ref, rwx_ref, rbx_ref, lam_ref,
                wgate_ref, bgate_ref, gng_ref, wbra_ref, wbrb_ref, wout_ref,
                rgbuf_ref, h_ref, s_ref, hist_ref, o_ref, *, tt, chunk):
    hist = RG_CONV - 1

    @pl.when(first)
    def _():
        hist_ref[...] = jnp.zeros(hist_ref.shape, F32)
        hist_ref[V7X_SUBLANES - hist:V7X_SUBLANES, :] = rgbuf0_ref[...]
        h_ref[...] = h0_ref[...]
        s_ref[...] = s0_ref[...]

    xn = _rmsnorm(x, g1_ref[...]).astype(BF16)

    rg_x = _dot(xn, wa_ref[:, 0:D_RNN])
    glr = _dot(xn, wglr_ref[...])
    xc = _causal_conv(rg_x, hist_ref[...], cw_ref[...], cb_ref[...], hist)
    hist_ref[...] = rg_x[tt - V7X_SUBLANES:tt, :]
    rgbuf_ref[...] = hist_ref[V7X_SUBLANES - hist:V7X_SUBLANES, :]
    qk = _dot(xn, wa_ref[:, Q_OFF:V_OFF])
    a, b = _rglru_coeffs(xc, rwa_ref, rba_ref, rwx_ref, rbx_ref, lam_ref)
    rg_y = _dot(xn, wa_ref[:, D_RNN:2 * D_RNN])
    v = _dot(xn, wa_ref[:, V_OFF:V_OFF + V_DIM]).astype(BF16)
    h = _scan_affine(a, b, h_ref[...], tt)
    h_ref[...] = h[tt - 1:tt, :]
    zb = _dot(xn, wb_ref[...])
    ya = _dot((h * jax.nn.gelu(rg_y)).astype(BF16), wbra_ref[...])

    bc, causal, blocks = _gla_decay(glr, wgate_ref, bgate_ref, tt, chunk)
    _gla_heads(qk, v, bc, causal, blocks, s_ref, o_ref, chunk)
    return _mixer_tail(x, zb, ya, o_ref[...], gng_ref, wbrb_ref, wout_ref)


def _ffn_body(x, first, buf0_ref, g2_ref, wup_ref, cw_ref, cb_ref, wdn_ref, gf_ref,
              buf_ref, hist_ref, *, tt, cf, final):
    hist = FFN_CONV - 1

    @pl.when(first)
    def _():
        hist_ref[...] = jnp.zeros(hist_ref.shape, F32)
        hist_ref[V7X_SUBLANES - hist:V7X_SUBLANES, :] = buf0_ref[...]

    xn = _rmsnorm(x, g2_ref[...]).astype(BF16)

    def up_proj(j):
        return [(slice(off, off + cf), _dot(xn, wup_ref[:, off:off + cf]))
                for off in (j * cf, D_FF + j * cf)]

    acc = x
    n_chunks = D_FF // cf
    ups = up_proj(0)
    for j in range(n_chunks):
        nxt = up_proj(j + 1) if j + 1 < n_chunks else None
        halves = []
        for cs, u in ups:
            halves.append(_causal_conv(u, hist_ref[:, cs], cw_ref[:, cs], cb_ref[:, cs], hist))
            hist_ref[:, cs] = u[tt - V7X_SUBLANES:tt, :]
        hmid = (jax.nn.gelu(halves[0]) * halves[1]).astype(BF16)
        acc = acc + _dot(hmid, wdn_ref[j * cf:(j + 1) * cf, :])
        ups = nxt
    buf_ref[...] = hist_ref[V7X_SUBLANES - hist:V7X_SUBLANES, :]
    return _rmsnorm(acc, gf_ref[...]) if final else acc


N_MIXER_IN = 20
N_FFN_IN = 7


def _mixer_seq_kernel(x_ref, *refs, tt, chunk):
    ins, (xo_ref, rgbuf_ref, h_ref, s_ref, hist_ref, o_ref) = refs[:N_MIXER_IN], refs[N_MIXER_IN:]
    xo_ref[...] = _mixer_body(x_ref[...], pl.program_id(1) == 0, *ins,
                              rgbuf_ref, h_ref, s_ref, hist_ref, o_ref, tt=tt, chunk=chunk)


def _ffn_seq_kernel(x_ref, *refs, tt, cf, final):
    ins, (xo_ref, buf_ref, hist_ref) = refs[:N_FFN_IN], refs[N_FFN_IN:]
    xo_ref[...] = _ffn_body(x_ref[...], pl.program_id(1) == 0, *ins, buf_ref, hist_ref,
                            tt=tt, cf=cf, final=final)


def _layer_seq_kernel(x_ref, *refs, tt, chunk, cf, final):
    m_in, f_in = refs[:N_MIXER_IN], refs[N_MIXER_IN:N_MIXER_IN + N_FFN_IN]
    (xo_ref, rgbuf_ref, h_ref, s_ref, buf_ref,
     hist_ref, o_ref, fhist_ref) = refs[N_MIXER_IN + N_FFN_IN:]
    first = pl.program_id(1) == 0
    x1 = _mixer_body(x_ref[...], first, *m_in, rgbuf_ref, h_ref, s_ref, hist_ref, o_ref,
                     tt=tt, chunk=chunk)
    xo_ref[...] = _ffn_body(x1, first, *f_in, buf_ref, fhist_ref, tt=tt, cf=cf, final=final)


def _layer_spec(tail, l, single=True):
    zeros = (0,) * len(tail)
    kw = dict(pipeline_mode=pl.Buffered(1)) if single else {}
    return pl.BlockSpec((None,) + tuple(tail), lambda *_: (l,) + zeros, **kw)


def _mixer_seq(x, rgbuf0, h0, s0, w, l, tt, chunk):
    bsz, t_len, _ = x.shape
    bi = 1 if rgbuf0.shape[0] == bsz else 0

    def state_spec(tail):
        zeros = (0,) * len(tail)
        return pl.BlockSpec((None,) + tuple(tail), lambda b, t: (b * bi,) + zeros)

    def out_state_spec(tail):
        zeros = (0,) * len(tail)
        return pl.BlockSpec((None,) + tuple(tail), lambda b, t: (b,) + zeros)

    x_spec = pl.BlockSpec((None, tt, D_MODEL), lambda b, t: (b, t, 0))
    hist = RG_CONV - 1
    names = ("norm1_g", "w_a", "w_glr", "w_b", "rg_conv_w", "rg_conv_b", "rg_wa", "rg_ba",
             "rg_wx", "rg_bx", "rg_lambda", "gla_w_gate", "gla_b_gate", "gla_norm_g",
             "w_branch_a", "w_branch_b", "w_out")
    return pl.pallas_call(
        functools.partial(_mixer_seq_kernel, tt=tt, chunk=chunk),
        grid=(bsz, t_len // tt),
        in_specs=[x_spec, state_spec((hist, D_RNN)), state_spec((1, D_RNN)),
                  state_spec((GLA_HEADS, GLA_DK, GLA_DV))]
                 + [_layer_spec(w[n].shape[1:], l) for n in names],
        out_specs=[x_spec, out_state_spec((hist, D_RNN)), out_state_spec((1, D_RNN)),
                   out_state_spec((GLA_HEADS, GLA_DK, GLA_DV))],
        out_shape=[jax.ShapeDtypeStruct(x.shape, F32),
                   jax.ShapeDtypeStruct((bsz, hist, D_RNN), F32),
                   jax.ShapeDtypeStruct((bsz, 1, D_RNN), F32),
                   jax.ShapeDtypeStruct((bsz, GLA_HEADS, GLA_DK, GLA_DV), F32)],
        scratch_shapes=[pltpu.VMEM((V7X_SUBLANES, D_RNN), F32),
                        pltpu.VMEM((tt, V_DIM), F32)],
        compiler_params=pltpu.CompilerParams(
            dimension_semantics=("arbitrary", "arbitrary"), vmem_limit_bytes=VMEM_LIMIT),
        name="mixer_seq",
    )(x, rgbuf0, h0, s0, *[w[n] for n in names])


def _ffn_seq(x, buf0, w, l, tt, final):
    bsz, t_len, _ = x.shape
    bi = 1 if buf0.shape[0] == bsz else 0
    hist = FFN_CONV - 1
    cf = FFN_COL_TILE
    x_spec = pl.BlockSpec((None, tt, D_MODEL), lambda b, t: (b, t, 0))
    names = ("norm2_g", "w_ffn_up", "ffn_conv_w", "ffn_conv_b", "w_ffn_down")
    return pl.pallas_call(
        functools.partial(_ffn_seq_kernel, tt=tt, cf=cf, final=final),
        grid=(bsz, t_len // tt),
        in_specs=[x_spec, pl.BlockSpec((None, hist, 2 * D_FF), lambda b, t: (b * bi, 0, 0))]
                 + [_layer_spec(w[n].shape[1:], l) for n in names]
                 + [pl.BlockSpec((1, D_MODEL), lambda b, t: (0, 0))],
        out_specs=[x_spec, pl.BlockSpec((None, hist, 2 * D_FF), lambda b, t: (b, 0, 0))],
        out_shape=[jax.ShapeDtypeStruct(x.shape, F32),
                   jax.ShapeDtypeStruct((bsz, hist, 2 * D_FF), F32)],
        scratch_shapes=[pltpu.VMEM((V7X_SUBLANES, 2 * D_FF), F32)],
        compiler_params=pltpu.CompilerParams(
            dimension_semantics=("arbitrary", "arbitrary"), vmem_limit_bytes=VMEM_LIMIT),
        name="ffn_seq",
    )(x, buf0, *[w[n] for n in names], w["final_norm_g"])


MIXER_WEIGHTS = ("norm1_g", "w_a", "w_glr", "w_b", "rg_conv_w", "rg_conv_b", "rg_wa", "rg_ba",
                 "rg_wx", "rg_bx", "rg_lambda", "gla_w_gate", "gla_b_gate", "gla_norm_g",
                 "w_branch_a", "w_branch_b", "w_out")
FFN_WEIGHTS = ("norm2_g", "w_ffn_up", "ffn_conv_w", "ffn_conv_b", "w_ffn_down")


def _layer_seq(x, rgbuf0, h0, s0, fbuf0, w, l, tt, chunk, final):
    bsz, t_len, _ = x.shape
    bi = 1 if rgbuf0.shape[0] == bsz else 0
    hist, fhist = RG_CONV - 1, FFN_CONV - 1

    def state_spec(tail, per_row):
        zeros = (0,) * len(tail)
        return pl.BlockSpec((None,) + tuple(tail), lambda b, t: (b * per_row,) + zeros)

    x_spec = pl.BlockSpec((None, tt, D_MODEL), lambda b, t: (b, t, 0))
    state_tails = [(hist, D_RNN), (1, D_RNN), (GLA_HEADS, GLA_DK, GLA_DV)]
    ffn_tail = (fhist, 2 * D_FF)
    return pl.pallas_call(
        functools.partial(_layer_seq_kernel, tt=tt, chunk=chunk, cf=FFN_COL_TILE, final=final),
        grid=(bsz, t_len // tt),
        in_specs=[x_spec] + [state_spec(t, bi) for t in state_tails]
                 + [_layer_spec(w[n].shape[1:], l) for n in MIXER_WEIGHTS]
                 + [state_spec(ffn_tail, bi)]
                 + [_layer_spec(w[n].shape[1:], l) for n in FFN_WEIGHTS]
                 + [pl.BlockSpec((1, D_MODEL), lambda b, t: (0, 0))],
        out_specs=[x_spec] + [state_spec(t, 1) for t in state_tails] + [state_spec(ffn_tail, 1)],
        out_shape=[jax.ShapeDtypeStruct(x.shape, F32)]
                  + [jax.ShapeDtypeStruct((bsz,) + t, F32) for t in state_tails]
                  + [jax.ShapeDtypeStruct((bsz,) + ffn_tail, F32)],
        scratch_shapes=[pltpu.VMEM((V7X_SUBLANES, D_RNN), F32),
                        pltpu.VMEM((tt, V_DIM), F32),
                        pltpu.VMEM((V7X_SUBLANES, 2 * D_FF), F32)],
        compiler_params=pltpu.CompilerParams(
            dimension_semantics=("arbitrary", "arbitrary"), vmem_limit_bytes=VMEM_LIMIT),
        name="layer_seq",
    )(x, rgbuf0, h0, s0, *[w[n] for n in MIXER_WEIGHTS], fbuf0,
      *[w[n] for n in FFN_WEIGHTS], w["final_norm_g"])


def _trunk_seq(x, states, w, tt, ffn_tt, chunk):
    outs = []
    for l in range(DEPTH):
        rgbuf0, h0, s0, fbuf0 = states[l]
        final = l == DEPTH - 1
        if x.shape[1] == tt:
            x, rgbuf, h, s, fbuf = _layer_seq(x, rgbuf0, h0, s0, fbuf0, w, l, tt, chunk, final)
        else:
            x, rgbuf, h, s = _mixer_seq(x, rgbuf0, h0, s0, w, l, tt, chunk)
            x, fbuf = _ffn_seq(x, fbuf0, w, l, ffn_tt, final=final)
        outs.append((rgbuf, h, s, fbuf))
    return x, outs


def _sample_pre_kernel(x_ref, rb0_ref, rb1_ref, rb2_ref, h0_ref, g1_ref, wa_ref, wglr_ref, wb_ref,
                       cw_ref, cb_ref, rwa_ref, rba_ref, rwx_ref, rbx_ref, lam_ref,
                       wgate_ref, bgate_ref, wbra_ref,
                       rgx_ref, h_ref, ya_ref, at_ref, kt_ref, wt_ref, v_ref, okv_ref, zb_ref):
    hist = RG_CONV - 1
    rb_refs = (rb0_ref, rb1_ref, rb2_ref)
    xn = _rmsnorm(x_ref[...], g1_ref[...]).astype(BF16)
    za = _dot(xn, wa_ref[...])
    glr = _dot(xn, wglr_ref[...])
    zb_ref[...] = _dot(xn, wb_ref[...])

    rg_x = za[:, 0:D_RNN]
    rgx_ref[...] = rg_x
    cw = cw_ref[...]
    xc = cb_ref[...] + rb_refs[0][...] * cw[0:1]
    for j in range(1, hist):
        xc = xc + rb_refs[j][...] * cw[j:j + 1]
    xc = xc + rg_x * cw[hist:hist + 1]
    a, b = _rglru_coeffs(xc, rwa_ref, rba_ref, rwx_ref, rbx_ref, lam_ref)
    h = a * h0_ref[...] + b
    h_ref[...] = h
    ya_ref[...] = _dot((h * jax.nn.gelu(za[:, D_RNN:2 * D_RNN])).astype(BF16), wbra_ref[...])

    zg = _dot(glr.astype(BF16), wgate_ref[...]) + bgate_ref[...]
    alpha = jnp.exp(-_softplus(-zg) * (1.0 / GLA_TAU))
    q = za[:, Q_OFF:Q_OFF + QK_DIM] * (GLA_DK ** -0.5)
    k = za[:, K_OFF:K_OFF + QK_DIM]
    v = za[:, V_OFF:V_OFF + V_DIM]
    at_ref[...] = alpha.T
    kt_ref[...] = k.T
    wt_ref[...] = (q * alpha).T
    v_ref[...] = v
    qk = q * k
    okv_ref[...] = jnp.concatenate(
        [jnp.sum(qk[:, hh * GLA_DK:(hh + 1) * GLA_DK], axis=-1, keepdims=True)
         * v[:, hh * GLA_DV:(hh + 1) * GLA_DV] for hh in range(GLA_HEADS)], axis=1)


def _row_onehot(nb, row):
    rows = lax.broadcasted_iota(jnp.int32, (nb, GLA_DV), 0)
    return jnp.where(rows == row, 1.0, 0.0).astype(BF16)


def _sample_read_kernel(s0_ref, wt_ref, okv_ref, o_ref, *, nb):
    i = pl.program_id(0)
    wb = wt_ref[...].astype(BF16)
    for j in range(SAMPLE_ROWS):
        cw = _dot(wb, _row_onehot(nb, i * SAMPLE_ROWS + j))
        for hh in range(GLA_HEADS):
            ks = slice(hh * GLA_DK, (hh + 1) * GLA_DK)
            vs = slice(hh * GLA_DV, (hh + 1) * GLA_DV)
            o_ref[j:j + 1, vs] = (jnp.sum(cw[ks] * s0_ref[j, hh], axis=0, keepdims=True)
                                  + okv_ref[j:j + 1, vs])


def _sample_update_kernel(s0_ref, at_ref, kt_ref, v_ref, s_ref, *, nb):
    i = pl.program_id(1)
    al = at_ref[...]
    a_hi = al.astype(BF16)
    r1 = al - a_hi.astype(F32)
    a_mid = r1.astype(BF16)
    a_lo = (r1 - a_mid.astype(F32)).astype(BF16)
    a_top = jnp.concatenate([a_hi, a_mid], axis=1)
    kb = kt_ref[...].astype(BF16)
    for j in range(SAMPLE_ROWS):
        e = _row_onehot(nb, i * SAMPLE_ROWS + j)
        ca = _dot(a_top, jnp.concatenate([e, e], axis=0)) + _dot(a_lo, e)
        ck = _dot(kb, e)
        for hh in range(GLA_HEADS):
            ks = slice(hh * GLA_DK, (hh + 1) * GLA_DK)
            vs = slice(hh * GLA_DV, (hh + 1) * GLA_DV)
            s_ref[j, hh] = ca[ks] * s0_ref[j, hh] + ck[ks] * v_ref[j:j + 1, vs]


def _sample_post_kernel(x_ref, ya_ref, o_ref, zb_ref, gng_ref, wbrb_ref, wout_ref, g2_ref,
                        wupg_ref, wupv_ref, cwg_ref, cwv_ref, cbg_ref, cbv_ref,
                        bg0_ref, bv0_ref, bg1_ref, bv1_ref, wdn_ref, gf_ref,
                        xo_ref, ug_ref, uv_ref, xn_ref, *, final):
    j = pl.program_id(0)

    @pl.when(j == 0)
    def _():
        zb = zb_ref[...]
        on = _head_norm_gate(o_ref[...], gng_ref[...], zb[:, 0:V_DIM])
        yb = _dot(on.astype(BF16), wbrb_ref[...])
        merged = (jax.nn.sigmoid(zb[:, V_DIM:V_DIM + D_MODEL]) * ya_ref[...]
                  + jax.nn.sigmoid(zb[:, V_DIM + D_MODEL:V_DIM + 2 * D_MODEL]) * yb)
        x1 = x_ref[...] + _dot(merged.astype(BF16), wout_ref[...])
        xo_ref[...] = x1
        xn_ref[...] = _rmsnorm(x1, g2_ref[...]).astype(BF16)

    xn = xn_ref[...]
    ug = _dot(xn, wupg_ref[...])
    uv = _dot(xn, wupv_ref[...])
    ug_ref[...] = ug
    uv_ref[...] = uv
    cwg = cwg_ref[...]
    cwv = cwv_ref[...]
    gate = cbg_ref[...] + bg0_ref[...] * cwg[0:1] + bg1_ref[...] * cwg[1:2] + ug * cwg[2:3]
    val = cbv_ref[...] + bv0_ref[...] * cwv[0:1] + bv1_ref[...] * cwv[1:2] + uv * cwv[2:3]
    xo_ref[...] += _dot((jax.nn.gelu(gate) * val).astype(BF16), wdn_ref[...])

    if final:
        @pl.when(j == pl.num_programs(0) - 1)
        def _():
            xo_ref[...] = _rmsnorm(xo_ref[...], gf_ref[...])


def _full_spec(shape):
    zeros = (0,) * len(shape)
    return pl.BlockSpec(tuple(shape), lambda *_: zeros)


def _sample_layer(x, rg_rows, h_all, s_all, ffn_rows, w, l, final):
    nb = x.shape[0]
    params = pltpu.CompilerParams(dimension_semantics=("arbitrary",),
                                  vmem_limit_bytes=VMEM_LIMIT)

    pre_names = ("norm1_g", "w_a", "w_glr", "w_b", "rg_conv_w", "rg_conv_b", "rg_wa", "rg_ba",
                 "rg_wx", "rg_bx", "rg_lambda", "gla_w_gate", "gla_b_gate", "w_branch_a")
    pre_out = [(nb, D_RNN), (nb, D_RNN), (nb, D_MODEL), (QK_DIM, nb), (QK_DIM, nb), (QK_DIM, nb),
               (nb, V_DIM), (nb, V_DIM), (nb, W_B_COLS)]
    rg_x, h, ya, a_t, k_t, w_t, v, okv, zb = pl.pallas_call(
        _sample_pre_kernel,
        grid=(1,),
        in_specs=[_full_spec(x.shape)]
                 + [_layer_spec(r.shape[1:], l, single=False) for r in rg_rows]
                 + [_layer_spec(h_all.shape[1:], l, single=False)]
                 + [_layer_spec(w[n].shape[1:], l) for n in pre_names],
        out_specs=[_full_spec(s) for s in pre_out],
        out_shape=[jax.ShapeDtypeStruct(s, F32) for s in pre_out],
        compiler_params=params,
        name="sample_pre",
    )(x, *rg_rows, h_all, *[w[n] for n in pre_names])

    s_blk = (None, SAMPLE_ROWS, GLA_HEADS, GLA_DK, GLA_DV)
    row_spec = pl.BlockSpec((SAMPLE_ROWS, V_DIM), lambda i: (i, 0))
    o = pl.pallas_call(
        functools.partial(_sample_read_kernel, nb=nb),
        grid=(nb // SAMPLE_ROWS,),
        in_specs=[pl.BlockSpec(s_blk, lambda i: (l, i, 0, 0, 0)), _full_spec((QK_DIM, nb)),
                  row_spec],
        out_specs=row_spec,
        out_shape=jax.ShapeDtypeStruct((nb, V_DIM), F32),
        compiler_params=params,
        name="sample_read",
    )(s_all, w_t, okv)

    tf = SAMPLE_FF_TILE
    nf = D_FF // tf

    def lcol(rows, off):
        return pl.BlockSpec((None, rows, tf), lambda j: (l, 0, off + j))

    ucol = pl.BlockSpec((nb, tf), lambda j: (0, j))
    xo, ug, uv = pl.pallas_call(
        functools.partial(_sample_post_kernel, final=final),
        grid=(nf,),
        in_specs=[_full_spec(x.shape), _full_spec(ya.shape), _full_spec(o.shape),
                  _full_spec(zb.shape),
                  _layer_spec(w["gla_norm_g"].shape[1:], l, single=False),
                  _layer_spec(w["w_branch_b"].shape[1:], l, single=False),
                  _layer_spec(w["w_out"].shape[1:], l, single=False),
                  _layer_spec(w["norm2_g"].shape[1:], l, single=False),
                  lcol(D_MODEL, 0), lcol(D_MODEL, nf),
                  lcol(FFN_CONV, 0), lcol(FFN_CONV, nf),
                  lcol(1, 0), lcol(1, nf),
                  lcol(nb, 0), lcol(nb, nf), lcol(nb, 0), lcol(nb, nf),
                  pl.BlockSpec((None, tf, D_MODEL), lambda j: (l, j, 0)),
                  _full_spec((1, D_MODEL))],
        out_specs=[_full_spec(x.shape), ucol, ucol],
        out_shape=[jax.ShapeDtypeStruct(x.shape, F32),
                   jax.ShapeDtypeStruct((nb, D_FF), F32), jax.ShapeDtypeStruct((nb, D_FF), F32)],
        scratch_shapes=[pltpu.VMEM((nb, D_MODEL), BF16)],
        compiler_params=params,
        name="sample_post",
    )(x, ya, o, zb, w["gla_norm_g"], w["w_branch_b"], w["w_out"], w["norm2_g"],
      w["w_ffn_up"], w["w_ffn_up"], w["ffn_conv_w"], w["ffn_conv_w"],
      w["ffn_conv_b"], w["ffn_conv_b"], ffn_rows[0], ffn_rows[0], ffn_rows[1], ffn_rows[1],
      w["w_ffn_down"], w["final_norm_g"])
    return xo, rg_x, h, (a_t, k_t, v), (ug, uv)


def _sample_state_update(s_all, a_t, k_t, v):
    nb = v.shape[1]
    s_blk = (None, SAMPLE_ROWS, GLA_HEADS, GLA_DK, GLA_DV)
    s_spec = pl.BlockSpec(s_blk, lambda l, i: (l, i, 0, 0, 0))
    t_spec = pl.BlockSpec((None, QK_DIM, nb), lambda l, i: (l, 0, 0))
    return pl.pallas_call(
        functools.partial(_sample_update_kernel, nb=nb),
        grid=(DEPTH, nb // SAMPLE_ROWS),
        in_specs=[s_spec, t_spec, t_spec,
                  pl.BlockSpec((None, SAMPLE_ROWS, V_DIM), lambda l, i: (l, i, 0))],
        out_specs=s_spec,
        out_shape=jax.ShapeDtypeStruct(s_all.shape, F32),
        compiler_params=pltpu.CompilerParams(
            dimension_semantics=("arbitrary", "arbitrary"), vmem_limit_bytes=VMEM_LIMIT),
        name="sample_update",
    )(s_all, a_t, k_t, v)


def _prep_weights(norm1_g, w_in, rg_conv_w, rg_conv_b, rg_wa, rg_ba, rg_wx, rg_bx, rg_lambda,
                  gla_w_gate, gla_b_gate, gla_norm_g, w_branch_a, w_branch_b, w_out, norm2_g,
                  w_ffn_up, ffn_conv_w, ffn_conv_b, w_ffn_down, final_norm_g):
    def vec(a):
        return a.reshape(DEPTH, 1, -1).astype(F32)

    def block_diag(wb):
        per = V7X_MXU_DIM // RG_BW
        w5 = wb.reshape(DEPTH, RG_BLOCKS // per, per, RG_BW, RG_BW)
        bd = jnp.einsum("lgicd,ij->lgicjd", w5, jnp.eye(per, dtype=wb.dtype))
        return bd.reshape(DEPTH, RG_BLOCKS // per, V7X_MXU_DIM, V7X_MXU_DIM).astype(BF16)

    pad_r = V7X_LANES - GLA_RANK
    return dict(
        norm1_g=vec(norm1_g),
        w_a=w_in[:, :, 0:W_A_COLS].astype(BF16),
        w_glr=jnp.pad(w_in[:, :, W_GLR_OFF:W_B_OFF], ((0, 0), (0, 0), (0, pad_r))).astype(BF16),
        w_b=w_in[:, :, W_B_OFF:W_B_OFF + W_B_COLS].astype(BF16),
        rg_conv_w=rg_conv_w.astype(F32), rg_conv_b=vec(rg_conv_b),
        rg_wa=block_diag(rg_wa), rg_ba=vec(rg_ba), rg_wx=block_diag(rg_wx), rg_bx=vec(rg_bx),
        rg_lambda=vec(rg_lambda),
        gla_w_gate=jnp.pad(gla_w_gate, ((0, 0), (0, pad_r), (0, 0))).astype(BF16),
        gla_b_gate=vec(gla_b_gate), gla_norm_g=vec(gla_norm_g),
        w_branch_a=w_branch_a.astype(BF16), w_branch_b=w_branch_b.astype(BF16),
        w_out=w_out.astype(BF16), norm2_g=vec(norm2_g),
        w_ffn_up=w_ffn_up.astype(BF16), ffn_conv_w=ffn_conv_w.astype(F32),
        ffn_conv_b=vec(ffn_conv_b), w_ffn_down=w_ffn_down.astype(BF16),
        final_norm_g=final_norm_g.reshape(1, D_MODEL).astype(F32),
    )


def kernel(x_prompt, x_sample, state_rg_conv, state_rg_h, state_gla, state_ffn_conv, meta_tokens, norm1_g, w_in, rg_conv_w, rg_conv_b, rg_wa, rg_ba, rg_wx, rg_bx, rg_lambda, gla_w_gate, gla_b_gate, gla_norm_g, w_branch_a, w_branch_b, w_out, norm2_g, w_ffn_up, ffn_conv_w, ffn_conv_b, w_ffn_down, final_norm_g):
    w = _prep_weights(norm1_g, w_in, rg_conv_w, rg_conv_b, rg_wa, rg_ba, rg_wx, rg_bx, rg_lambda,
                      gla_w_gate, gla_b_gate, gla_norm_g, w_branch_a, w_branch_b, w_out, norm2_g,
                      w_ffn_up, ffn_conv_w, ffn_conv_b, w_ffn_down, final_norm_g)
    bsz = x_prompt.shape[0]

    zero_states = [(jnp.zeros((1, RG_CONV - 1, D_RNN), F32), jnp.zeros((1, 1, D_RNN), F32),
                    jnp.zeros((1, GLA_HEADS, GLA_DK, GLA_DV), F32),
                    jnp.zeros((1, FFN_CONV - 1, 2 * D_FF), F32))] * DEPTH
    _, meta_states = _trunk_seq(meta_tokens.astype(F32)[None], zero_states, w,
                                tt=N_META, ffn_tt=N_META, chunk=N_META)

    y_prompt, p_states = _trunk_seq(x_prompt, meta_states, w, tt=SEQ_TILE, ffn_tt=FFN_SEQ_TILE,
                                    chunk=GLA_CHUNK)
    rg_conv_prompt = jnp.stack([s[0] for s in p_states])
    rg_h_prompt = jnp.stack([s[1].reshape(bsz, D_RNN) for s in p_states])
    gla_prompt = jnp.stack([s[2] for s in p_states])
    ffn_conv_prompt = jnp.stack([s[3] for s in p_states])

    nb = x_sample.shape[0]
    xs = x_sample.reshape(nb, D_MODEL)
    rg_rows = [state_rg_conv[:, :, j] for j in range(RG_CONV - 1)]
    ffn_rows = [state_ffn_conv[:, :, j] for j in range(FFN_CONV - 1)]
    rg_xs, hs, akv, us = [], [], [], []
    for l in range(DEPTH):
        xs, rg_x, h, akv_l, u_l = _sample_layer(xs, rg_rows, state_rg_h, state_gla, ffn_rows,
                                                w, l, final=(l == DEPTH - 1))
        rg_xs.append(rg_x)
        hs.append(h)
        akv.append(akv_l)
        us.append(jnp.concatenate(u_l, axis=1))
    gla_sample = _sample_state_update(state_gla, *[jnp.stack(t) for t in zip(*akv)])
    rg_conv_sample = jnp.stack(rg_rows[1:] + [jnp.stack(rg_xs)], axis=2)
    ffn_conv_sample = jnp.stack(ffn_rows[1:] + [jnp.stack(us)], axis=2)
    y_sample = xs.reshape(x_sample.shape)
    return (y_prompt, y_sample, rg_conv_prompt, rg_h_prompt, gla_prompt, ffn_conv_prompt,
            rg_conv_sample, jnp.stack(hs), gla_sample, ffn_conv_sample)
```

```python
import functools

import jax
import jax.numpy as jnp
from jax import lax
from jax.experimental import pallas as pl
from jax.experimental.pallas import tpu as pltpu

F32 = jnp.float32
BF16 = jnp.bfloat16

D_MODEL = 1024
DEPTH = 4
D_RNN = D_MODEL
RG_BLOCKS = 16
RG_BW = D_RNN // RG_BLOCKS
RG_CONV = 4
RG_C = 8.0
GLA_HEADS = 4
GLA_DK = 128
GLA_DV = 256
QK_DIM = GLA_HEADS * GLA_DK
V_DIM = GLA_HEADS * GLA_DV
GLA_RANK = 16
GLA_TAU = 16.0
GLA_CHUNK = 64
D_FF = 3 * D_MODEL
FFN_CONV = 3
N_META = 16
EPS = 1e-6

V7X_MXU_DIM = 256
V7X_LANES = 128
V7X_SUBLANES = 8

W_A_COLS = 2 * D_RNN + 2 * QK_DIM + V_DIM
W_GLR_OFF = W_A_COLS
W_B_OFF = W_A_COLS + GLA_RANK
W_B_COLS = V_DIM + 2 * D_MODEL
Q_OFF, K_OFF, V_OFF = 2 * D_RNN, 2 * D_RNN + QK_DIM, 2 * D_RNN + 2 * QK_DIM

SEQ_TILE = 512
FFN_SEQ_TILE = 512
FFN_COL_TILE = 1024
SAMPLE_ROWS = 16
SAMPLE_FF_TILE = 512
VMEM_LIMIT = 56 * 1024 * 1024


def _dot(a, b):
    return jnp.dot(a, b, preferred_element_type=F32)


def _rmsnorm(x, g):
    ms = jnp.mean(x * x, axis=-1, keepdims=True)
    return x * lax.rsqrt(ms + EPS) * g


def _softplus(x):
    return jnp.maximum(x, 0.0) + jnp.log1p(jnp.exp(-jnp.abs(x)))


def _block_diag_dot(xb, w_ref):
    n = D_RNN // V7X_MXU_DIM
    return jnp.concatenate(
        [_dot(xb[:, i * V7X_MXU_DIM:(i + 1) * V7X_MXU_DIM], w_ref[i]) for i in range(n)], axis=1)


def _rglru_coeffs(xc, rwa_ref, rba_ref, rwx_ref, rbx_ref, lam_ref):
    xcb = xc.astype(BF16)
    r = jax.nn.sigmoid(_block_diag_dot(xcb, rwa_ref) + rba_ref[...])
    i = jax.nn.sigmoid(_block_diag_dot(xcb, rwx_ref) + rbx_ref[...])
    log_a = (-RG_C) * r * _softplus(-lam_ref[...])
    a = jnp.exp(log_a)
    y = 1.0 - a * a
    b = jnp.where(y > 0.0, y * lax.rsqrt(y), 0.0) * (i * xc)
    return a, b


def _head_norm_gate(o, gng, g_out):
    parts = []
    for hh in range(GLA_HEADS):
        sl = slice(hh * GLA_DV, (hh + 1) * GLA_DV)
        oh = o[:, sl]
        ms = jnp.mean(oh * oh, axis=-1, keepdims=True)
        parts.append(oh * lax.rsqrt(ms + EPS) * gng[:, sl])
    on = jnp.concatenate(parts, axis=1)
    return on * (g_out * jax.nn.sigmoid(g_out))


def _delayed_rows(prev, u, s):
    ext = jnp.concatenate([prev, u], axis=0)
    return pltpu.roll(ext, s, axis=0)[V7X_SUBLANES:]


def _causal_conv(u, prev, cw, cb, hist):
    y = cb + _delayed_rows(prev, u, hist) * cw[0:1]
    for j in range(1, hist):
        y = y + _delayed_rows(prev, u, hist - j) * cw[j:j + 1]
    return y + u * cw[hist:hist + 1]


def _scan_affine(a, b, h_init, tt):
    width = a.shape[1]
    n_tiles = tt // V7X_SUBLANES
    a = a.reshape(n_tiles, V7X_SUBLANES, width)
    b = b.reshape(n_tiles, V7X_SUBLANES, width)
    sub = lax.broadcasted_iota(jnp.int32, a.shape, 1)
    s = 1
    while s < V7X_SUBLANES:
        keep = sub >= s
        b = jnp.where(keep, a * pltpu.roll(b, s, axis=1) + b, b)
        a = jnp.where(keep, a * pltpu.roll(a, s, axis=1), a)
        s *= 2
    tiles = []
    carry = h_init
    for g in range(n_tiles):
        h = b[g] + a[g] * carry
        carry = h[V7X_SUBLANES - 1:V7X_SUBLANES, :]
        tiles.append(h)
    return jnp.concatenate(tiles, axis=0)


def _gla_decay(glr, wgate_ref, bgate_ref, tt, chunk):
    zg = _dot(glr.astype(BF16), wgate_ref[...]) + bgate_ref[...]
    la = -_softplus(-zg) * (1.0 / GLA_TAU)
    blk = min(tt, V7X_MXU_DIM)
    blocks = [slice(r0, r0 + blk) for r0 in range(0, tt, blk)]
    shift = chunk.bit_length() - 1
    r_i = lax.broadcasted_iota(jnp.int32, (blk, blk), 0)
    c_i = lax.broadcasted_iota(jnp.int32, (blk, blk), 1)
    causal = ((r_i >> shift) == (c_i >> shift)) & (c_i <= r_i)
    cum = jnp.where(causal, 1.0, 0.0).astype(BF16)
    la_hi = la.astype(BF16)
    la_lo = (la - la_hi.astype(F32)).astype(BF16)
    bc = jnp.concatenate([_dot(cum, la_hi[rb]) + _dot(cum, la_lo[rb]) for rb in blocks], axis=0)
    return bc, causal, blocks


def _gla_heads(qk, v, bc, causal, blocks, s_ref, o_ref, chunk):
    blk = blocks[0].stop
    qd_all = (qk[:, 0:QK_DIM] * (GLA_DK ** -0.5) * jnp.exp(bc)).astype(BF16)
    kd_all = (qk[:, QK_DIM:2 * QK_DIM] * jnp.exp(-bc)).astype(BF16)
    for hh in range(GLA_HEADS):
        ks = slice(hh * GLA_DK, (hh + 1) * GLA_DK)
        vs = slice(hh * GLA_DV, (hh + 1) * GLA_DV)
        for rb in blocks:
            att = lax.dot_general(qd_all[rb, ks], kd_all[rb, ks], (((1,), (1,)), ((), ())),
                                  preferred_element_type=F32)
            o_intra = _dot(jnp.where(causal, att, 0.0).astype(BF16), v[rb, vs])
            for c0 in range(0, blk, chunk):
                rs = slice(rb.start + c0, rb.start + c0 + chunk)
                g = bc[rs, ks]
                gl = g[chunk - 1:chunk, :]
                ke = (qk[rs, QK_DIM + hh * GLA_DK:QK_DIM + (hh + 1) * GLA_DK]
                      * jnp.exp(gl - g)).astype(BF16)
                st = s_ref[hh]
                o_ref[rs, vs] = o_intra[c0:c0 + chunk] + _dot(qd_all[rs, ks], st.astype(BF16))
                ds = lax.dot_general(ke, v[rs, vs], (((0,), (0,)), ((), ())),
                                     preferred_element_type=F32)
                dec = jnp.broadcast_to(jnp.exp(gl), (GLA_DK, GLA_DK)).T
                s_ref[hh] = jnp.concatenate([dec] * (GLA_DV // GLA_DK), axis=1) * st + ds


def _mixer_tail(x, zb, ya, o, gng_ref, wbrb_ref, wout_ref):
    on = _head_norm_gate(o, gng_ref[...], zb[:, 0:V_DIM])
    yb = _dot(on.astype(BF16), wbrb_ref[...])
    merged = (jax.nn.sigmoid(zb[:, V_DIM:V_DIM + D_MODEL]) * ya
              + jax.nn.sigmoid(zb[:, V_DIM + D_MODEL:V_DIM + 2 * D_MODEL]) * yb)
    return x + _dot(merged.astype(BF16), wout_ref[...])


def _mixer_body(x, first, rgbuf0_ref, h0_ref, s0_ref, g1_ref, wa_ref, wglr_ref, wb_ref,
                cw_ref, cb_ref, rwa_ref, rba_ref, rwx_ref, rbx_ref, lam_ref,
                wgate_ref, bgate_ref, gng_ref, wbra_ref, wbrb_ref, wout_ref,
                rgbuf_ref, h_ref, s_ref, hist_ref, o_ref, *, tt, chunk):
    hist = RG_CONV - 1

    @pl.when(first)
    def _():
        hist_ref[...] = jnp.zeros(hist_ref.shape, F32)
        hist_ref[V7X_SUBLANES - hist:V7X_SUBLANES, :] = rgbuf0_ref[...]
        h_ref[...] = h0_ref[...]
        s_ref[...] = s0_ref[...]

    xn = _rmsnorm(x, g1_ref[...]).astype(BF16)

    rg_x = _dot(xn, wa_ref[:, 0:D_RNN])
    glr = _dot(xn, wglr_ref[...])
    xc = _causal_conv(rg_x, hist_ref[...], cw_ref[...], cb_ref[...], hist)
    hist_ref[...] = rg_x[tt - V7X_SUBLANES:tt, :]
    rgbuf_ref[...] = hist_ref[V7X_SUBLANES - hist:V7X_SUBLANES, :]
    qk = _dot(xn, wa_ref[:, Q_OFF:V_OFF])
    a, b = _rglru_coeffs(xc, rwa_ref, rba_ref, rwx_ref, rbx_ref, lam_ref)
    rg_y = _dot(xn, wa_ref[:, D_RNN:2 * D_RNN])
    v = _dot(xn, wa_ref[:, V_OFF:V_OFF + V_DIM]).astype(BF16)
    h = _scan_affine(a, b, h_ref[...], tt)
    h_ref[...] = h[tt - 1:tt, :]
    zb = _dot(xn, wb_ref[...])
    ya = _dot((h * jax.nn.gelu(rg_y)).astype(BF16), wbra_ref[...])

    bc, causal, blocks = _gla_decay(glr, wgate_ref, bgate_ref, tt, chunk)
    _gla_heads(qk, v, bc, causal, blocks, s_ref, o_ref, chunk)
    return _mixer_tail(x, zb, ya, o_ref[...], gng_ref, wbrb_ref, wout_ref)


def _ffn_body(x, first, buf0_ref, g2_ref, wup_ref, cw_ref, cb_ref, wdn_ref, gf_ref,
              buf_ref, hist_ref, *, tt, cf, final):
    hist = FFN_CONV - 1

    @pl.when(first)
    def _():
        hist_ref[...] = jnp.zeros(hist_ref.shape, F32)
        hist_ref[V7X_SUBLANES - hist:V7X_SUBLANES, :] = buf0_ref[...]

    xn = _rmsnorm(x, g2_ref[...]).astype(BF16)

    def up_proj(j):
        return [(slice(off, off + cf), _dot(xn, wup_ref[:, off:off + cf]))
                for off in (j * cf, D_FF + j * cf)]

    acc = x
    n_chunks = D_FF // cf
    ups = up_proj(0)
    for j in range(n_chunks):
        nxt = up_proj(j + 1) if j + 1 < n_chunks else None
        halves = []
        for cs, u in ups:
            halves.append(_causal_conv(u, hist_ref[:, cs], cw_ref[:, cs], cb_ref[:, cs], hist))
            hist_ref[:, cs] = u[tt - V7X_SUBLANES:tt, :]
        hmid = (jax.nn.gelu(halves[0]) * halves[1]).astype(BF16)
        acc = acc + _dot(hmid, wdn_ref[j * cf:(j + 1) * cf, :])
        ups = nxt
    buf_ref[...] = hist_ref[V7X_SUBLANES - hist:V7X_SUBLANES, :]
    return _rmsnorm(acc, gf_ref[...]) if final else acc


N_MIXER_IN = 20
N_FFN_IN = 7


def _mixer_seq_kernel(x_ref, *refs, tt, chunk):
    ins, (xo_ref, rgbuf_ref, h_ref, s_ref, hist_ref, o_ref) = refs[:N_MIXER_IN], refs[N_MIXER_IN:]
    xo_ref[...] = _mixer_body(x_ref[...], pl.program_id(1) == 0, *ins,
                              rgbuf_ref, h_ref, s_ref, hist_ref, o_ref, tt=tt, chunk=chunk)


def _ffn_seq_kernel(x_ref, *refs, tt, cf, final):
    ins, (xo_ref, buf_ref, hist_ref) = refs[:N_FFN_IN], refs[N_FFN_IN:]
    xo_ref[...] = _ffn_body(x_ref[...], pl.program_id(1) == 0, *ins, buf_ref, hist_ref,
                            tt=tt, cf=cf, final=final)


def _layer_seq_kernel(x_ref, *refs, tt, chunk, cf, final):
    m_in, f_in = refs[:N_MIXER_IN], refs[N_MIXER_IN:N_MIXER_IN + N_FFN_IN]
    (xo_ref, rgbuf_ref, h_ref, s_ref, buf_ref,
     hist_ref, o_ref, fhist_ref) = refs[N_MIXER_IN + N_FFN_IN:]
    first = pl.program_id(1) == 0
    x1 = _mixer_body(x_ref[...], first, *m_in, rgbuf_ref, h_ref, s_ref, hist_ref, o_ref,
                     tt=tt, chunk=chunk)
    xo_ref[...] = _ffn_body(x1, first, *f_in, buf_ref, fhist_ref, tt=tt, cf=cf, final=final)


def _layer_spec(tail, l, single=True):
    zeros = (0,) * len(tail)
    kw = dict(pipeline_mode=pl.Buffered(1)) if single else {}
    return pl.BlockSpec((None,) + tuple(tail), lambda *_: (l,) + zeros, **kw)


def _mixer_seq(x, rgbuf0, h0, s0, w, l, tt, chunk):
    bsz, t_len, _ = x.shape
    bi = 1 if rgbuf0.shape[0] == bsz else 0

    def state_spec(tail):
        zeros = (0,) * len(tail)
        return pl.BlockSpec((None,) + tuple(tail), lambda b, t: (b * bi,) + zeros)

    def out_state_spec(tail):
        zeros = (0,) * len(tail)
        return pl.BlockSpec((None,) + tuple(tail), lambda b, t: (b,) + zeros)

    x_spec = pl.BlockSpec((None, tt, D_MODEL), lambda b, t: (b, t, 0))
    hist = RG_CONV - 1
    names = ("norm1_g", "w_a", "w_glr", "w_b", "rg_conv_w", "rg_conv_b", "rg_wa", "rg_ba",
             "rg_wx", "rg_bx", "rg_lambda", "gla_w_gate", "gla_b_gate", "gla_norm_g",
             "w_branch_a", "w_branch_b", "w_out")
    return pl.pallas_call(
        functools.partial(_mixer_seq_kernel, tt=tt, chunk=chunk),
        grid=(bsz, t_len // tt),
        in_specs=[x_spec, state_spec((hist, D_RNN)), state_spec((1, D_RNN)),
                  state_spec((GLA_HEADS, GLA_DK, GLA_DV))]
                 + [_layer_spec(w[n].shape[1:], l) for n in names],
        out_specs=[x_spec, out_state_spec((hist, D_RNN)), out_state_spec((1, D_RNN)),
                   out_state_spec((GLA_HEADS, GLA_DK, GLA_DV))],
        out_shape=[jax.ShapeDtypeStruct(x.shape, F32),
                   jax.ShapeDtypeStruct((bsz, hist, D_RNN), F32),
                   jax.ShapeDtypeStruct((bsz, 1, D_RNN), F32),
                   jax.ShapeDtypeStruct((bsz, GLA_HEADS, GLA_DK, GLA_DV), F32)],
        scratch_shapes=[pltpu.VMEM((V7X_SUBLANES, D_RNN), F32),
                        pltpu.VMEM((tt, V_DIM), F32)],
        compiler_params=pltpu.CompilerParams(
            dimension_semantics=("arbitrary", "arbitrary"), vmem_limit_bytes=VMEM_LIMIT),
        name="mixer_seq",
    )(x, rgbuf0, h0, s0, *[w[n] for n in names])


def _ffn_seq(x, buf0, w, l, tt, final):
    bsz, t_len, _ = x.shape
    bi = 1 if buf0.shape[0] == bsz else 0
    hist = FFN_CONV - 1
    cf = FFN_COL_TILE
    x_spec = pl.BlockSpec((None, tt, D_MODEL), lambda b, t: (b, t, 0))
    names = ("norm2_g", "w_ffn_up", "ffn_conv_w", "ffn_conv_b", "w_ffn_down")
    return pl.pallas_call(
        functools.partial(_ffn_seq_kernel, tt=tt, cf=cf, final=final),
        grid=(bsz, t_len // tt),
        in_specs=[x_spec, pl.BlockSpec((None, hist, 2 * D_FF), lambda b, t: (b * bi, 0, 0))]
                 + [_layer_spec(w[n].shape[1:], l) for n in names]
                 + [pl.BlockSpec((1, D_MODEL), lambda b, t: (0, 0))],
        out_specs=[x_spec, pl.BlockSpec((None, hist, 2 * D_FF), lambda b, t: (b, 0, 0))],
        out_shape=[jax.ShapeDtypeStruct(x.shape, F32),
                   jax.ShapeDtypeStruct((bsz, hist, 2 * D_FF), F32)],
        scratch_shapes=[pltpu.VMEM((V7X_SUBLANES, 2 * D_FF), F32)],
        compiler_params=pltpu.CompilerParams(
            dimension_semantics=("arbitrary", "arbitrary"), vmem_limit_bytes=VMEM_LIMIT),
        name="ffn_seq",
    )(x, buf0, *[w[n] for n in names], w["final_norm_g"])


MIXER_WEIGHTS = ("norm1_g", "w_a", "w_glr", "w_b", "rg_conv_w", "rg_conv_b", "rg_wa", "rg_ba",
                 "rg_wx", "rg_bx", "rg_lambda", "gla_w_gate", "gla_b_gate", "gla_norm_g",
                 "w_branch_a", "w_branch_b", "w_out")
FFN_WEIGHTS = ("norm2_g", "w_ffn_up", "ffn_conv_w", "ffn_conv_b", "w_ffn_down")


def _layer_seq(x, rgbuf0, h0, s0, fbuf0, w, l, tt, chunk, final):
    bsz, t_len, _ = x.shape
    bi = 1 if rgbuf0.shape[0] == bsz else 0
    hist, fhist = RG_CONV - 1, FFN_CONV - 1

    def state_spec(tail, per_row):
        zeros = (0,) * len(tail)
        return pl.BlockSpec((None,) + tuple(tail), lambda b, t: (b * per_row,) + zeros)

    x_spec = pl.BlockSpec((None, tt, D_MODEL), lambda b, t: (b, t, 0))
    state_tails = [(hist, D_RNN), (1, D_RNN), (GLA_HEADS, GLA_DK, GLA_DV)]
    ffn_tail = (fhist, 2 * D_FF)
    return pl.pallas_call(
        functools.partial(_layer_seq_kernel, tt=tt, chunk=chunk, cf=FFN_COL_TILE, final=final),
        grid=(bsz, t_len // tt),
        in_specs=[x_spec] + [state_spec(t, bi) for t in state_tails]
                 + [_layer_spec(w[n].shape[1:], l) for n in MIXER_WEIGHTS]
                 + [state_spec(ffn_tail, bi)]
                 + [_layer_spec(w[n].shape[1:], l) for n in FFN_WEIGHTS]
                 + [pl.BlockSpec((1, D_MODEL), lambda b, t: (0, 0))],
        out_specs=[x_spec] + [state_spec(t, 1) for t in state_tails] + [state_spec(ffn_tail, 1)],
        out_shape=[jax.ShapeDtypeStruct(x.shape, F32)]
                  + [jax.ShapeDtypeStruct((bsz,) + t, F32) for t in state_tails]
                  + [jax.ShapeDtypeStruct((bsz,) + ffn_tail, F32)],
        scratch_shapes=[pltpu.VMEM((V7X_SUBLANES, D_RNN), F32),
                        pltpu.VMEM((tt, V_DIM), F32),
                        pltpu.VMEM((V7X_SUBLANES, 2 * D_FF), F32)],
        compiler_params=pltpu.CompilerParams(
            dimension_semantics=("arbitrary", "arbitrary"), vmem_limit_bytes=VMEM_LIMIT),
        name="layer_seq",
    )(x, rgbuf0, h0, s0, *[w[n] for n in MIXER_WEIGHTS], fbuf0,
      *[w[n] for n in FFN_WEIGHTS], w["final_norm_g"])


def _trunk_seq(x, states, w, tt, ffn_tt, chunk):
    outs = []
    for l in range(DEPTH):
        rgbuf0, h0, s0, fbuf0 = states[l]
        final = l == DEPTH - 1
        if x.shape[1] == tt:
            x, rgbuf, h, s, fbuf = _layer_seq(x, rgbuf0, h0, s0, fbuf0, w, l, tt, chunk, final)
        else:
            x, rgbuf, h, s = _mixer_seq(x, rgbuf0, h0, s0, w, l, tt, chunk)
            x, fbuf = _ffn_seq(x, fbuf0, w, l, ffn_tt, final=final)
        outs.append((rgbuf, h, s, fbuf))
    return x, outs


def _sample_pre_kernel(x_ref, rb0_ref, rb1_ref, rb2_ref, h0_ref, g1_ref, wa_ref, wglr_ref, wb_ref,
                       cw_ref, cb_ref, rwa_ref, rba_ref, rwx_ref, rbx_ref, lam_ref,
                       wgate_ref, bgate_ref, wbra_ref,
                       rgx_ref, h_ref, ya_ref, at_ref, kt_ref, wt_ref, v_ref, okv_ref, zb_ref):
    hist = RG_CONV - 1
    rb_refs = (rb0_ref, rb1_ref, rb2_ref)
    xn = _rmsnorm(x_ref[...], g1_ref[...]).astype(BF16)
    za = _dot(xn, wa_ref[...])
    glr = _dot(xn, wglr_ref[...])
    zb_ref[...] = _dot(xn, wb_ref[...])

    rg_x = za[:, 0:D_RNN]
    rgx_ref[...] = rg_x
    cw = cw_ref[...]
    xc = cb_ref[...] + rb_refs[0][...] * cw[0:1]
    for j in range(1, hist):
        xc = xc + rb_refs[j][...] * cw[j:j + 1]
    xc = xc + rg_x * cw[hist:hist + 1]
    a, b = _rglru_coeffs(xc, rwa_ref, rba_ref, rwx_ref, rbx_ref, lam_ref)
    h = a * h0_ref[...] + b
    h_ref[...] = h
    ya_ref[...] = _dot((h * jax.nn.gelu(za[:, D_RNN:2 * D_RNN])).astype(BF16), wbra_ref[...])

    zg = _dot(glr.astype(BF16), wgate_ref[...]) + bgate_ref[...]
    alpha = jnp.exp(-_softplus(-zg) * (1.0 / GLA_TAU))
    q = za[:, Q_OFF:Q_OFF + QK_DIM] * (GLA_DK ** -0.5)
    k = za[:, K_OFF:K_OFF + QK_DIM]
    v = za[:, V_OFF:V_OFF + V_DIM]
    at_ref[...] = alpha.T
    kt_ref[...] = k.T
    wt_ref[...] = (q * alpha).T
    v_ref[...] = v
    qk = q * k
    okv_ref[...] = jnp.concatenate(
        [jnp.sum(qk[:, hh * GLA_DK:(hh + 1) * GLA_DK], axis=-1, keepdims=True)
         * v[:, hh * GLA_DV:(hh + 1) * GLA_DV] for hh in range(GLA_HEADS)], axis=1)


def _row_onehot(nb, row):
    rows = lax.broadcasted_iota(jnp.int32, (nb, GLA_DV), 0)
    return jnp.where(rows == row, 1.0, 0.0).astype(BF16)


def _sample_read_kernel(s0_ref, wt_ref, okv_ref, o_ref, *, nb):
    i = pl.program_id(0)
    wb = wt_ref[...].astype(BF16)
    for j in range(SAMPLE_ROWS):
        cw = _dot(wb, _row_onehot(nb, i * SAMPLE_ROWS + j))
        for hh in range(GLA_HEADS):
            ks = slice(hh * GLA_DK, (hh + 1) * GLA_DK)
            vs = slice(hh * GLA_DV, (hh + 1) * GLA_DV)
            o_ref[j:j + 1, vs] = (jnp.sum(cw[ks] * s0_ref[j, hh], axis=0, keepdims=True)
                                  + okv_ref[j:j + 1, vs])


def _sample_update_kernel(s0_ref, at_ref, kt_ref, v_ref, s_ref, *, nb):
    i = pl.program_id(1)
    al = at_ref[...]
    a_hi = al.astype(BF16)
    r1 = al - a_hi.astype(F32)
    a_mid = r1.astype(BF16)
    a_lo = (r1 - a_mid.astype(F32)).astype(BF16)
    a_top = jnp.concatenate([a_hi, a_mid], axis=1)
    kb = kt_ref[...].astype(BF16)
    for j in range(SAMPLE_ROWS):
        e = _row_onehot(nb, i * SAMPLE_ROWS + j)
        ca = _dot(a_top, jnp.concatenate([e, e], axis=0)) + _dot(a_lo, e)
        ck = _dot(kb, e)
        for hh in range(GLA_HEADS):
            ks = slice(hh * GLA_DK, (hh + 1) * GLA_DK)
            vs = slice(hh * GLA_DV, (hh + 1) * GLA_DV)
            s_ref[j, hh] = ca[ks] * s0_ref[j, hh] + ck[ks] * v_ref[j:j + 1, vs]


def _sample_post_kernel(x_ref, ya_ref, o_ref, zb_ref, gng_ref, wbrb_ref, wout_ref, g2_ref,
                        wupg_ref, wupv_ref, cwg_ref, cwv_ref, cbg_ref, cbv_ref,
                        bg0_ref, bv0_ref, bg1_ref, bv1_ref, wdn_ref, gf_ref,
                        xo_ref, ug_ref, uv_ref, xn_ref, *, final):
    j = pl.program_id(0)

    @pl.when(j == 0)
    def _():
        zb = zb_ref[...]
        on = _head_norm_gate(o_ref[...], gng_ref[...], zb[:, 0:V_DIM])
        yb = _dot(on.astype(BF16), wbrb_ref[...])
        merged = (jax.nn.sigmoid(zb[:, V_DIM:V_DIM + D_MODEL]) * ya_ref[...]
                  + jax.nn.sigmoid(zb[:, V_DIM + D_MODEL:V_DIM + 2 * D_MODEL]) * yb)
        x1 = x_ref[...] + _dot(merged.astype(BF16), wout_ref[...])
        xo_ref[...] = x1
        xn_ref[...] = _rmsnorm(x1, g2_ref[...]).astype(BF16)

    xn = xn_ref[...]
    ug = _dot(xn, wupg_ref[...])
    uv = _dot(xn, wupv_ref[...])
    ug_ref[...] = ug
    uv_ref[...] = uv
    cwg = cwg_ref[...]
    cwv = cwv_ref[...]
    gate = cbg_ref[...] + bg0_ref[...] * cwg[0:1] + bg1_ref[...] * cwg[1:2] + ug * cwg[2:3]
    val = cbv_ref[...] + bv0_ref[...] * cwv[0:1] + bv1_ref[...] * cwv[1:2] + uv * cwv[2:3]
    xo_ref[...] += _dot((jax.nn.gelu(gate) * val).astype(BF16), wdn_ref[...])

    if final:
        @pl.when(j == pl.num_programs(0) - 1)
        def _():
            xo_ref[...] = _rmsnorm(xo_ref[...], gf_ref[...])


def _full_spec(shape):
    zeros = (0,) * len(shape)
    return pl.BlockSpec(tuple(shape), lambda *_: zeros)


def _sample_layer(x, rg_rows, h_all, s_all, ffn_rows, w, l, final):
    nb = x.shape[0]
    params = pltpu.CompilerParams(dimension_semantics=("arbitrary",),
                                  vmem_limit_bytes=VMEM_LIMIT)

    pre_names = ("norm1_g", "w_a", "w_glr", "w_b", "rg_conv_w", "rg_conv_b", "rg_wa", "rg_ba",
                 "rg_wx", "rg_bx", "rg_lambda", "gla_w_gate", "gla_b_gate", "w_branch_a")
    pre_out = [(nb, D_RNN), (nb, D_RNN), (nb, D_MODEL), (QK_DIM, nb), (QK_DIM, nb), (QK_DIM, nb),
               (nb, V_DIM), (nb, V_DIM), (nb, W_B_COLS)]
    rg_x, h, ya, a_t, k_t, w_t, v, okv, zb = pl.pallas_call(
        _sample_pre_kernel,
        grid=(1,),
        in_specs=[_full_spec(x.shape)]
                 + [_layer_spec(r.shape[1:], l, single=False) for r in rg_rows]
                 + [_layer_spec(h_all.shape[1:], l, single=False)]
                 + [_layer_spec(w[n].shape[1:], l) for n in pre_names],
        out_specs=[_full_spec(s) for s in pre_out],
        out_shape=[jax.ShapeDtypeStruct(s, F32) for s in pre_out],
        compiler_params=params,
        name="sample_pre",
    )(x, *rg_rows, h_all, *[w[n] for n in pre_names])

    s_blk = (None, SAMPLE_ROWS, GLA_HEADS, GLA_DK, GLA_DV)
    row_spec = pl.BlockSpec((SAMPLE_ROWS, V_DIM), lambda i: (i, 0))
    o = pl.pallas_call(
        functools.partial(_sample_read_kernel, nb=nb),
        grid=(nb // SAMPLE_ROWS,),
        in_specs=[pl.BlockSpec(s_blk, lambda i: (l, i, 0, 0, 0)), _full_spec((QK_DIM, nb)),
                  row_spec],
        out_specs=row_spec,
        out_shape=jax.ShapeDtypeStruct((nb, V_DIM), F32),
        compiler_params=params,
        name="sample_read",
    )(s_all, w_t, okv)

    tf = SAMPLE_FF_TILE
    nf = D_FF // tf

    def lcol(rows, off):
        return pl.BlockSpec((None, rows, tf), lambda j: (l, 0, off + j))

    ucol = pl.BlockSpec((nb, tf), lambda j: (0, j))
    xo, ug, uv = pl.pallas_call(
        functools.partial(_sample_post_kernel, final=final),
        grid=(nf,),
        in_specs=[_full_spec(x.shape), _full_spec(ya.shape), _full_spec(o.shape),
                  _full_spec(zb.shape),
                  _layer_spec(w["gla_norm_g"].shape[1:], l, single=False),
                  _layer_spec(w["w_branch_b"].shape[1:], l, single=False),
                  _layer_spec(w["w_out"].shape[1:], l, single=False),
                  _layer_spec(w["norm2_g"].shape[1:], l, single=False),
                  lcol(D_MODEL, 0), lcol(D_MODEL, nf),
                  lcol(FFN_CONV, 0), lcol(FFN_CONV, nf),
                  lcol(1, 0), lcol(1, nf),
                  lcol(nb, 0), lcol(nb, nf), lcol(nb, 0), lcol(nb, nf),
                  pl.BlockSpec((None, tf, D_MODEL), lambda j: (l, j, 0)),
                  _full_spec((1, D_MODEL))],
        out_specs=[_full_spec(x.shape), ucol, ucol],
        out_shape=[jax.ShapeDtypeStruct(x.shape, F32),
                   jax.ShapeDtypeStruct((nb, D_FF), F32), jax.ShapeDtypeStruct((nb, D_FF), F32)],
        scratch_shapes=[pltpu.VMEM((nb, D_MODEL), BF16)],
        compiler_params=params,
        name="sample_post",
    )(x, ya, o, zb, w["gla_norm_g"], w["w_branch_b"], w["w_out"], w["norm2_g"],
      w["w_ffn_up"], w["w_ffn_up"], w["ffn_conv_w"], w["ffn_conv_w"],
      w["ffn_conv_b"], w["ffn_conv_b"], ffn_rows[0], ffn_rows[0], ffn_rows[1], ffn_rows[1],
      w["w_ffn_down"], w["final_norm_g"])
    return xo, rg_x, h, (a_t, k_t, v), (ug, uv)


def _sample_state_update(s_all, a_t, k_t, v):
    nb = v.shape[1]
    s_blk = (None, SAMPLE_ROWS, GLA_HEADS, GLA_DK, GLA_DV)
    s_spec = pl.BlockSpec(s_blk, lambda l, i: (l, i, 0, 0, 0))
    t_spec = pl.BlockSpec((None, QK_DIM, nb), lambda l, i: (l, 0, 0))
    return pl.pallas_call(
        functools.partial(_sample_update_kernel, nb=nb),
        grid=(DEPTH, nb // SAMPLE_ROWS),
        in_specs=[s_spec, t_spec, t_spec,
                  pl.BlockSpec((None, SAMPLE_ROWS, V_DIM), lambda l, i: (l, i, 0))],
        out_specs=s_spec,
        out_shape=jax.ShapeDtypeStruct(s_all.shape, F32),
        compiler_params=pltpu.CompilerParams(
            dimension_semantics=("arbitrary", "arbitrary"), vmem_limit_bytes=VMEM_LIMIT),
        name="sample_update",
    )(s_all, a_t, k_t, v)


def _prep_weights(norm1_g, w_in, rg_conv_w, rg_conv_b, rg_wa, rg_ba, rg_wx, rg_bx, rg_lambda,
                  gla_w_gate, gla_b_gate, gla_norm_g, w_branch_a, w_branch_b, w_out, norm2_g,
                  w_ffn_up, ffn_conv_w, ffn_conv_b, w_ffn_down, final_norm_g):
    def vec(a):
        return a.reshape(DEPTH, 1, -1).astype(F32)

    def block_diag(wb):
        per = V7X_MXU_DIM // RG_BW
        w5 = wb.reshape(DEPTH, RG_BLOCKS // per, per, RG_BW, RG_BW)
        bd = jnp.einsum("lgicd,ij->lgicjd", w5, jnp.eye(per, dtype=wb.dtype))
        return bd.reshape(DEPTH, RG_BLOCKS // per, V7X_MXU_DIM, V7X_MXU_DIM).astype(BF16)

    pad_r = V7X_LANES - GLA_RANK
    return dict(
        norm1_g=vec(norm1_g),
        w_a=w_in[:, :, 0:W_A_COLS].astype(BF16),
        w_glr=jnp.pad(w_in[:, :, W_GLR_OFF:W_B_OFF], ((0, 0), (0, 0), (0, pad_r))).astype(BF16),
        w_b=w_in[:, :, W_B_OFF:W_B_OFF + W_B_COLS].astype(BF16),
        rg_conv_w=rg_conv_w.astype(F32), rg_conv_b=vec(rg_conv_b),
        rg_wa=block_diag(rg_wa), rg_ba=vec(rg_ba), rg_wx=block_diag(rg_wx), rg_bx=vec(rg_bx),
        rg_lambda=vec(rg_lambda),
        gla_w_gate=jnp.pad(gla_w_gate, ((0, 0), (0, pad_r), (0, 0))).astype(BF16),
        gla_b_gate=vec(gla_b_gate), gla_norm_g=vec(gla_norm_g),
        w_branch_a=w_branch_a.astype(BF16), w_branch_b=w_branch_b.astype(BF16),
        w_out=w_out.astype(BF16), norm2_g=vec(norm2_g),
        w_ffn_up=w_ffn_up.astype(BF16), ffn_conv_w=ffn_conv_w.astype(F32),
        ffn_conv_b=vec(ffn_conv_b), w_ffn_down=w_ffn_down.astype(BF16),
        final_norm_g=final_norm_g.reshape(1, D_MODEL).astype(F32),
    )


def kernel(x_prompt, x_sample, state_rg_conv, state_rg_h, state_gla, state_ffn_conv, meta_tokens, norm1_g, w_in, rg_conv_w, rg_conv_b, rg_wa, rg_ba, rg_wx, rg_bx, rg_lambda, gla_w_gate, gla_b_gate, gla_norm_g, w_branch_a, w_branch_b, w_out, norm2_g, w_ffn_up, ffn_conv_w, ffn_conv_b, w_ffn_down, final_norm_g):
    w = _prep_weights(norm1_g, w_in, rg_conv_w, rg_conv_b, rg_wa, rg_ba, rg_wx, rg_bx, rg_lambda,
                      gla_w_gate, gla_b_gate, gla_norm_g, w_branch_a, w_branch_b, w_out, norm2_g,
                      w_ffn_up, ffn_conv_w, ffn_conv_b, w_ffn_down, final_norm_g)
    bsz = x_prompt.shape[0]

    zero_states = [(jnp.zeros((1, RG_CONV - 1, D_RNN), F32), jnp.zeros((1, 1, D_RNN), F32),
                    jnp.zeros((1, GLA_HEADS, GLA_DK, GLA_DV), F32),
                    jnp.zeros((1, FFN_CONV - 1, 2 * D_FF), F32))] * DEPTH
    _, meta_states = _trunk_seq(meta_tokens.astype(F32)[None], zero_states, w,
                                tt=N_META, ffn_tt=N_META, chunk=N_META)

    y_prompt, p_states = _trunk_seq(x_prompt, meta_states, w, tt=SEQ_TILE, ffn_tt=FFN_SEQ_TILE,
                                    chunk=GLA_CHUNK)
    rg_conv_prompt = jnp.stack([s[0] for s in p_states])
    rg_h_prompt = jnp.stack([s[1].reshape(bsz, D_RNN) for s in p_states])
    gla_prompt = jnp.stack([s[2] for s in p_states])
    ffn_conv_prompt = jnp.stack([s[3] for s in p_states])

    nb = x_sample.shape[0]
    xs = x_sample.reshape(nb, D_MODEL)
    rg_rows = [state_rg_conv[:, :, j] for j in range(RG_CONV - 1)]
    ffn_rows = [state_ffn_conv[:, :, j] for j in range(FFN_CONV - 1)]
    rg_xs, hs, akv, us = [], [], [], []
    for l in range(DEPTH):
        xs, rg_x, h, akv_l, u_l = _sample_layer(xs, rg_rows, state_rg_h, state_gla, ffn_rows,
                                                w, l, final=(l == DEPTH - 1))
        rg_xs.append(rg_x)
        hs.append(h)
        akv.append(akv_l)
        us.append(jnp.concatenate(u_l, axis=1))
    gla_sample = _sample_state_update(state_gla, *[jnp.stack(t) for t in zip(*akv)])
    rg_conv_sample = jnp.stack(rg_rows[1:] + [jnp.stack(rg_xs)], axis=2)
    ffn_conv_sample = jnp.stack(ffn_rows[1:] + [jnp.stack(us)], axis=2)
    y_sample = xs.reshape(x_sample.shape)
    return (y_prompt, y_sample, rg_conv_prompt, rg_h_prompt, gla_prompt, ffn_conv_prompt,
            rg_conv_sample, jnp.stack(hs), gla_sample, ffn_conv_sample)
```
